```python
import math
import jax, jax.numpy as jnp
from jax import lax
import numpy as np

D_MODEL = 1024
BATCH = 32
SEQ = 2048
DEPTH = 1

D_MIX = D_MODEL
HEAD_DIM = 64
N_MLA_HEADS = D_MIX // 2 // HEAD_DIM
N_DIL_HEADS = D_MIX // 2 // HEAD_DIM
MLA_NOPE_DIM = HEAD_DIM
MLA_ROPE_DIM = HEAD_DIM // 2
MLA_V_DIM = HEAD_DIM
MLA_Q_LORA = 3 * D_MODEL // 8
MLA_KV_LORA = D_MODEL // 4
D_MLA_OUT = N_MLA_HEADS * MLA_V_DIM
D_DIL_OUT = N_DIL_HEADS * HEAD_DIM
D_IN_PROJ = MLA_Q_LORA + MLA_KV_LORA + MLA_ROPE_DIM + 3 * D_DIL_OUT
IN_SPLITS = (MLA_Q_LORA, MLA_Q_LORA + MLA_KV_LORA, MLA_Q_LORA + MLA_KV_LORA + MLA_ROPE_DIM)
DIL_PAIRS = ((128, 1), (512, 4), (2048, 16))
PARTIAL_ROT_DIM = HEAD_DIM // 4
ROPE_THETA = 500000.0
D_FF = ((8 * D_MODEL // 3 + 255) // 256) * 256
Q_BLOCK = 128
NORM_EPS = 1e-6
MASK_VALUE = -1e30

kernel_name = "hybrid_mla_dilated_macaron_encoder"


def rms_norm(x, g):
    xf = x.astype(jnp.float32)
    y = xf * lax.rsqrt(jnp.mean(xf * xf, axis=-1, keepdims=True) + NORM_EPS)
    return (y * g.astype(jnp.float32)).astype(x.dtype)


def swiglu(x, w_gate, w_up, w_down):
    return (jax.nn.silu(x @ w_gate) * (x @ w_up)) @ w_down


def rope_tables(seq, rot_dim, dtype):
    half = rot_dim // 2
    inv_freq = ROPE_THETA ** (-jnp.arange(half, dtype=jnp.float32) * (2.0 / rot_dim))
    ang = jnp.arange(seq, dtype=jnp.float32)[:, None] * inv_freq[None, :]
    return jnp.cos(ang).astype(dtype), jnp.sin(ang).astype(dtype)


def apply_rope(x, cos, sin):
    half = cos.shape[-1]
    c = cos[:, None, :]
    s = sin[:, None, :]
    x1 = x[..., :half]
    x2 = x[..., half:2 * half]
    return jnp.concatenate([x1 * c - x2 * s, x2 * c + x1 * s, x[..., 2 * half:]], axis=-1)


def dense_attention(q, k, v):
    b, s, h, dk = q.shape
    nb = s // Q_BLOCK
    scale = dk ** -0.5
    qb = q.reshape(b, nb, Q_BLOCK, h, dk).transpose(1, 0, 2, 3, 4)

    def one_block(qblk):
        sc = jnp.einsum('bqhd,bkhd->bhqk', qblk, k, preferred_element_type=jnp.float32) * scale
        p = jax.nn.softmax(sc, axis=-1)
        return jnp.einsum('bhqk,bkhd->bqhd', p.astype(v.dtype), v)

    out = lax.map(one_block, qb)
    return out.transpose(1, 0, 2, 3, 4).reshape(b, s, h, v.shape[-1])


def banded_attention(q, k, v, half):
    L, dh = q.shape[-2], q.shape[-1]
    lead = q.shape[:-2]
    blk = min(Q_BLOCK, L)
    nb = -(-L // blk)
    lp = nb * blk
    span = blk + 2 * half
    pad_lead = [(0, 0)] * len(lead)
    qp = jnp.pad(q, pad_lead + [(0, lp - L), (0, 0)])
    kp = jnp.pad(k, pad_lead + [(half, lp - L + half), (0, 0)])
    vp = jnp.pad(v, pad_lead + [(half, lp - L + half), (0, 0)])
    idx = np.arange(nb)[:, None] * blk + np.arange(span)[None, :]
    kb = jnp.take(kp, idx, axis=-2)
    vb = jnp.take(vp, idx, axis=-2)
    qb = qp.reshape(*lead, nb, blk, dh)
    sc = jnp.einsum('...nqd,...nkd->...nqk', qb, kb, preferred_element_type=jnp.float32) * (dh ** -0.5)
    key_pos = idx - half
    t = np.arange(span)[None, :]
    qi = np.arange(blk)[:, None]
    band = (t >= qi) & (t <= qi + 2 * half)
    valid = (key_pos >= 0) & (key_pos < L)
    mask = band[None, :, :] & valid[:, None, :]
    sc = jnp.where(mask, sc, MASK_VALUE)
    m = jnp.max(sc, axis=-1, keepdims=True)
    p = jnp.exp(sc - m)
    denom = jnp.sum(p, axis=-1, keepdims=True)
    o = jnp.einsum('...nqk,...nkd->...nqd', (p / denom).astype(v.dtype), vb)
    lse = (m + jnp.log(denom))[..., 0]
    o = o.reshape(*lead, lp, dh)[..., :L, :]
    lse = lse.reshape(*lead, lp)[..., :L]
    return o, lse


def dilated_attention(q, k, v):
    b, s, h, dh = q.shape
    outs, lses = [], []
    for window, dil in DIL_PAIRS:
        L = s // dil

        def strided(t):
            return t.reshape(b, L, dil, h, dh).transpose(0, 2, 3, 1, 4)

        o, lse = banded_attention(strided(q), strided(k), strided(v), window // (2 * dil))
        outs.append(o.transpose(0, 3, 1, 2, 4).reshape(b, s, h, dh))
        lses.append(lse.transpose(0, 3, 1, 2).reshape(b, s, h))
    w = jax.nn.softmax(jnp.stack(lses, axis=0), axis=0)
    return jnp.einsum('nbsh,nbshd->bshd', w.astype(q.dtype), jnp.stack(outs, axis=0))


def _dense(key, fan_in, fan_out):
    return jax.random.normal(key, (DEPTH, fan_in, fan_out), jnp.float32) * fan_in ** -0.5


def _gain(key, dim):
    return 1.0 + 0.02 * jax.random.normal(key, (DEPTH, dim), jnp.float32)


def setup_inputs(seed: int = 0) -> dict:
    key = jax.random.key(seed)
    ks = jax.random.split(key, 20)
    return {
        "x": jax.random.normal(ks[0], (BATCH, SEQ, D_MODEL), jnp.float32),
        "ffn1_norm": _gain(ks[1], D_MODEL),
        "ffn1_w_gate": _dense(ks[2], D_MODEL, D_FF),
        "ffn1_w_up": _dense(ks[3], D_MODEL, D_FF),
        "ffn1_w_down": _dense(ks[4], D_FF, D_MODEL),
        "mix_norm": _gain(ks[5], D_MODEL),
        "w_in": _dense(ks[6], D_MODEL, D_IN_PROJ),
        "mla_q_norm": _gain(ks[7], MLA_Q_LORA),
        "mla_w_uq": _dense(ks[8], MLA_Q_LORA, N_MLA_HEADS * (MLA_NOPE_DIM + MLA_ROPE_DIM)),
        "mla_kv_norm": _gain(ks[9], MLA_KV_LORA),
        "mla_w_ukv": _dense(ks[10], MLA_KV_LORA, N_MLA_HEADS * (MLA_NOPE_DIM + MLA_V_DIM)),
        "mla_out_norm": _gain(ks[11], D_MLA_OUT),
        "dil_out_norm": _gain(ks[12], D_DIL_OUT),
        "w_out": _dense(ks[13], D_MIX, D_MODEL),
        "ffn2_norm": _gain(ks[14], D_MODEL),
        "ffn2_w_gate": _dense(ks[15], D_MODEL, D_FF),
        "ffn2_w_up": _dense(ks[16], D_MODEL, D_FF),
        "ffn2_w_down": _dense(ks[17], D_FF, D_MODEL),
        "final_norm": 1.0 + 0.02 * jax.random.normal(ks[18], (D_MODEL,), jnp.float32),
    }


def reference(x, ffn1_norm, ffn1_w_gate, ffn1_w_up, ffn1_w_down, mix_norm, w_in,
              mla_q_norm, mla_w_uq, mla_kv_norm, mla_w_ukv, mla_out_norm, dil_out_norm,
              w_out, ffn2_norm, ffn2_w_gate, ffn2_w_up, ffn2_w_down, final_norm):
    b, s, _ = x.shape
    cos_p, sin_p = rope_tables(s, PARTIAL_ROT_DIM, x.dtype)
    cos_m, sin_m = rope_tables(s, MLA_ROPE_DIM, x.dtype)
    for l in range(DEPTH):
        x = x + 0.5 * swiglu(rms_norm(x, ffn1_norm[l]), ffn1_w_gate[l], ffn1_w_up[l], ffn1_w_down[l])

        h = rms_norm(x, mix_norm[l])
        proj = h @ w_in[l]
        c_q, c_kv, k_rope, qkv_dil = jnp.split(proj, IN_SPLITS, axis=-1)

        q = (rms_norm(c_q, mla_q_norm[l]) @ mla_w_uq[l]).reshape(b, s, N_MLA_HEADS, MLA_NOPE_DIM + MLA_ROPE_DIM)
        q_mla = jnp.concatenate([q[..., :MLA_NOPE_DIM], apply_rope(q[..., MLA_NOPE_DIM:], cos_m, sin_m)], axis=-1)
        kv = (rms_norm(c_kv, mla_kv_norm[l]) @ mla_w_ukv[l]).reshape(b, s, N_MLA_HEADS, MLA_NOPE_DIM + MLA_V_DIM)
        k_nope = kv[..., :MLA_NOPE_DIM]
        v_mla = kv[..., MLA_NOPE_DIM:]
        k_r = apply_rope(k_rope[:, :, None, :], cos_m, sin_m)
        k_mla = jnp.concatenate([k_nope, jnp.broadcast_to(k_r, (b, s, N_MLA_HEADS, MLA_ROPE_DIM))], axis=-1)
        o_mla = dense_attention(q_mla, k_mla, v_mla).reshape(b, s, D_MLA_OUT)

        qkv = qkv_dil.reshape(b, s, 3, N_DIL_HEADS, HEAD_DIM)
        q_d = apply_rope(qkv[:, :, 0], cos_p, sin_p)
        k_d = apply_rope(qkv[:, :, 1], cos_p, sin_p)
        v_d = qkv[:, :, 2]
        o_dil = dilated_attention(q_d, k_d, v_d).reshape(b, s, D_DIL_OUT)

        mixed = jnp.concatenate([rms_norm(o_mla, mla_out_norm[l]), rms_norm(o_dil, dil_out_norm[l])], axis=-1)
        x = x + mixed @ w_out[l]

        x = x + 0.5 * swiglu(rms_norm(x, ffn2_norm[l]), ffn2_w_gate[l], ffn2_w_up[l], ffn2_w_down[l])
    return rms_norm(x, final_norm)
```

```python
import functools

import jax
import jax.numpy as jnp
from jax import lax
from jax.experimental import pallas as pl
from jax.experimental.pallas import tpu as pltpu

F32 = jnp.float32
BF16 = jnp.bfloat16

D_MODEL = 1024
HEAD_DIM = 64
N_MLA_HEADS = 8
N_DIL_HEADS = 8
MLA_NOPE_DIM = 64
MLA_ROPE_DIM = 32
MLA_V_DIM = 64
MLA_Q_LORA = 384
MLA_KV_LORA = 256
D_MLA_OUT = N_MLA_HEADS * MLA_V_DIM
D_DIL_OUT = N_DIL_HEADS * HEAD_DIM
DIL_PAIRS = ((128, 1), (512, 4), (2048, 16))
PARTIAL_ROT_DIM = 16
ROPE_THETA = 500000.0
D_FF = 2816
Q_BLOCK = 128
NORM_EPS = 1e-6
MASK_VALUE = -1e30
HALF_WINDOW = 64
assert all(w // (2 * d) == HALF_WINDOW for w, d in DIL_PAIRS)

LANES = 128
V7X_VMEM_LIMIT_BYTES = 56 * 1024 * 1024

FFN_ROWS = 512
PROJ_ROWS = 512
MLA_Q_ROWS = 512
PAIR = 2 * HEAD_DIM
N_PAIRS = N_MLA_HEADS // 2
MLA_PAIR_W = 2 * LANES

_OFF_CQ = 0
_OFF_CKV = _OFF_CQ + MLA_Q_LORA
_OFF_KR = _OFF_CKV + MLA_KV_LORA
_OFF_QD = _OFF_KR + LANES
_OFF_KD = _OFF_QD + D_DIL_OUT
_OFF_VD = _OFF_KD + D_DIL_OUT
_PROJ_W = _OFF_VD + D_DIL_OUT


def _rms(x, g):
    ms = jnp.mean(x * x, axis=-1, keepdims=True)
    return x * lax.rsqrt(ms + NORM_EPS) * g


def _rope(x, c, sa, sb, half):
    return x * c + pltpu.roll(x, LANES - half, 1) * sa + pltpu.roll(x, half, 1) * sb


def _swiglu_residual(x, g_ref, wg_ref, wu_ref, wd_ref):
    h = _rms(x, g_ref[...]).astype(BF16)
    gate = jnp.dot(h, wg_ref[...], preferred_element_type=F32)
    up = jnp.dot(h, wu_ref[...], preferred_element_type=F32)
    act = (gate * jax.nn.sigmoid(gate) * up).astype(BF16)
    return x + 0.5 * jnp.dot(act, wd_ref[...], preferred_element_type=F32)


def _ffn1_kernel(x_ref, g_ref, wg_ref, wu_ref, wd_ref, o_ref):
    o_ref[...] = _swiglu_residual(x_ref[...], g_ref, wg_ref, wu_ref, wd_ref)


def _ffn2_kernel(x_ref, om_ref, od_ref, gm_ref, gd_ref, woa_ref, wob_ref,
                 g_ref, wg_ref, wu_ref, wd_ref, gf_ref, o_ref):
    nm = _rms(om_ref[...].astype(F32), gm_ref[...]).astype(BF16)
    nd = _rms(od_ref[...].astype(F32), gd_ref[...]).astype(BF16)
    x = (x_ref[...]
         + jnp.dot(nm, woa_ref[...], preferred_element_type=F32)
         + jnp.dot(nd, wob_ref[...], preferred_element_type=F32))
    o_ref[...] = _rms(_swiglu_residual(x, g_ref, wg_ref, wu_ref, wd_ref), gf_ref[...])


def _resident(shape):
    return pl.BlockSpec(shape, lambda *_: (0,) * len(shape), pipeline_mode=pl.Buffered(1))


def _ffn1(x2d, g, wg, wu, wd):
    n = x2d.shape[0]
    row = pl.BlockSpec((FFN_ROWS, D_MODEL), lambda i: (i, 0))
    return pl.pallas_call(
        _ffn1_kernel,
        grid=(n // FFN_ROWS,),
        in_specs=[row, _resident((1, D_MODEL)), _resident((D_MODEL, D_FF)),
                  _resident((D_MODEL, D_FF)), _resident((D_FF, D_MODEL))],
        out_specs=row,
        out_shape=jax.ShapeDtypeStruct((n, D_MODEL), F32),
        compiler_params=pltpu.CompilerParams(
            dimension_semantics=("parallel",), vmem_limit_bytes=V7X_VMEM_LIMIT_BYTES),
        name="ffn1",
    )(x2d, g, wg, wu, wd)


def _ffn2(x2d, om, od, gm, gd, woa, wob, g, wg, wu, wd, gf):
    n = x2d.shape[0]
    row = pl.BlockSpec((FFN_ROWS, D_MODEL), lambda i: (i, 0))
    half_row = pl.BlockSpec((FFN_ROWS, D_MLA_OUT), lambda i: (i, 0))
    return pl.pallas_call(
        _ffn2_kernel,
        grid=(n // FFN_ROWS,),
        in_specs=[row, half_row, half_row,
                  _resident((1, D_MLA_OUT)), _resident((1, D_DIL_OUT)),
                  _resident((D_MLA_OUT, D_MODEL)), _resident((D_DIL_OUT, D_MODEL)),
                  _resident((1, D_MODEL)), _resident((D_MODEL, D_FF)),
                  _resident((D_MODEL, D_FF)), _resident((D_FF, D_MODEL)),
                  _resident((1, D_MODEL))],
        out_specs=row,
        out_shape=jax.ShapeDtypeStruct((n, D_MODEL), F32),
        compiler_params=pltpu.CompilerParams(
            dimension_semantics=("parallel",), vmem_limit_bytes=V7X_VMEM_LIMIT_BYTES),
        name="ffn2",
    )(x2d, om, od, gm, gd, woa, wob, g, wg, wu, wd, gf)


def _proj_kernel(x_ref, g_ref, win_ref, gq_ref, wuq_ref, gkv_ref, wukv_ref, tab_ref,
                 q_ref, kc_ref, v_ref, qd_ref, kd_ref, vd_ref):
    h = _rms(x_ref[...], g_ref[...]).astype(BF16)
    proj = jnp.dot(h, win_ref[...], preferred_element_type=F32)

    cm, sam, sbm = (tab_ref[:, i * LANES:(i + 1) * LANES] for i in range(3))
    cd, sad, sbd = (tab_ref[:, i * LANES:(i + 1) * LANES] for i in range(3, 6))
    rope_m = functools.partial(_rope, c=cm, sa=sam, sb=sbm, half=MLA_ROPE_DIM // 2)
    rope_d = functools.partial(_rope, c=cd, sa=sad, sb=sbd, half=PARTIAL_ROT_DIM // 2)

    cq = _rms(proj[:, _OFF_CQ:_OFF_CQ + MLA_Q_LORA], gq_ref[...]).astype(BF16)
    q = jnp.dot(cq, wuq_ref[...], preferred_element_type=F32)
    q_scale = (MLA_NOPE_DIM + MLA_ROPE_DIM) ** -0.5
    for p in range(N_PAIRS):
        lo = p * MLA_PAIR_W
        q_ref[:, lo:lo + LANES] = (q[:, lo:lo + LANES] * q_scale).astype(BF16)
        q_ref[:, lo + LANES:lo + 2 * LANES] = (
            rope_m(q[:, lo + LANES:lo + 2 * LANES]) * q_scale).astype(BF16)

    ckv = _rms(proj[:, _OFF_CKV:_OFF_CKV + MLA_KV_LORA], gkv_ref[...]).astype(BF16)
    kv = jnp.dot(ckv, wukv_ref[...], preferred_element_type=F32)
    k_rope = rope_m(proj[:, _OFF_KR:_OFF_KR + LANES]).astype(BF16)
    for p in range(N_PAIRS):
        lo = p * MLA_PAIR_W
        kc_ref[:, lo:lo + LANES] = kv[:, p * PAIR:(p + 1) * PAIR].astype(BF16)
        kc_ref[:, lo + LANES:lo + 2 * LANES] = k_rope
    v_ref[...] = kv[:, D_MLA_OUT:].astype(BF16)

    d_scale = HEAD_DIM ** -0.5
    for p in range(N_DIL_HEADS // 2):
        lo = p * PAIR
        qd_ref[:, lo:lo + PAIR] = rope_d(proj[:, _OFF_QD + lo:_OFF_QD + lo + PAIR]) * d_scale
        kd_ref[:, lo:lo + PAIR] = rope_d(proj[:, _OFF_KD + lo:_OFF_KD + lo + PAIR])
    vd_ref[...] = proj[:, _OFF_VD:_OFF_VD + D_DIL_OUT]


def _proj(x3d, g, win, gq, wuq, gkv, wukv, tab):
    b, s, _ = x3d.shape

    def tok(width):
        return pl.BlockSpec((None, PROJ_ROWS, width), lambda si, bi: (bi, si, 0))

    def out(width, dtype):
        return jax.ShapeDtypeStruct((b, s, width), dtype)

    return pl.pallas_call(
        _proj_kernel,
        grid=(s // PROJ_ROWS, b),
        in_specs=[tok(D_MODEL), _resident((1, D_MODEL)), _resident((D_MODEL, _PROJ_W)),
                  _resident((1, MLA_Q_LORA)), _resident((MLA_Q_LORA, N_PAIRS * MLA_PAIR_W)),
                  _resident((1, MLA_KV_LORA)), _resident((MLA_KV_LORA, 2 * D_MLA_OUT)),
                  pl.BlockSpec((PROJ_ROWS, 6 * LANES), lambda si, bi: (si, 0))],
        out_specs=[tok(N_PAIRS * MLA_PAIR_W), tok(N_PAIRS * MLA_PAIR_W), tok(D_MLA_OUT),
                   tok(D_DIL_OUT), tok(D_DIL_OUT), tok(D_DIL_OUT)],
        out_shape=[out(N_PAIRS * MLA_PAIR_W, BF16), out(N_PAIRS * MLA_PAIR_W, BF16),
                   out(D_MLA_OUT, BF16), out(D_DIL_OUT, F32), out(D_DIL_OUT, F32),
                   out(D_DIL_OUT, F32)],
        compiler_params=pltpu.CompilerParams(
            dimension_semantics=("parallel", "parallel"),
            vmem_limit_bytes=V7X_VMEM_LIMIT_BYTES),
        name="proj",
    )(x3d, g, win, gq, wuq, gkv, wukv, tab)


def _mla_kernel(q_ref, kc_ref, v_ref, o_ref):
    q = q_ref[...]
    kc = kc_ref[...]
    v = v_ref[...]
    lane = lax.broadcasted_iota(jnp.int32, q.shape, 1)
    outs = []
    for h in range(2):
        nope = (lane >= h * MLA_NOPE_DIM) & (lane < (h + 1) * MLA_NOPE_DIM)
        rope = ((lane >= LANES + h * MLA_ROPE_DIM) & (lane < LANES + (h + 1) * MLA_ROPE_DIM))
        qh = jnp.where(nope | rope, q, jnp.zeros_like(q))
        s = lax.dot_general(qh, kc, (((1,), (1,)), ((), ())), preferred_element_type=F32)
        m = jnp.max(s, axis=-1, keepdims=True)
        p = jnp.exp(s - m)
        l = jnp.sum(p, axis=-1, keepdims=True)
        outs.append(jnp.dot(p.astype(BF16), v, preferred_element_type=F32) / l)
    lane_o = lax.broadcasted_iota(jnp.int32, outs[0].shape, 1)
    o_ref[...] = jnp.where(lane_o < MLA_V_DIM, outs[0], outs[1]).astype(o_ref.dtype)


def _mla(q, kc, v):
    b, s, _ = q.shape
    return pl.pallas_call(
        _mla_kernel,
        grid=(b, N_PAIRS, s // MLA_Q_ROWS),
        in_specs=[pl.BlockSpec((None, MLA_Q_ROWS, MLA_PAIR_W), lambda bi, p, i: (bi, i, p)),
                  pl.BlockSpec((None, s, MLA_PAIR_W), lambda bi, p, i: (bi, 0, p)),
                  pl.BlockSpec((None, s, PAIR), lambda bi, p, i: (bi, 0, p))],
        out_specs=pl.BlockSpec((None, MLA_Q_ROWS, PAIR), lambda bi, p, i: (bi, i, p)),
        out_shape=jax.ShapeDtypeStruct((b, s, D_MLA_OUT), BF16),
        compiler_params=pltpu.CompilerParams(
            dimension_semantics=("parallel", "parallel", "arbitrary"),
            vmem_limit_bytes=V7X_VMEM_LIMIT_BYTES),
        name="mla",
    )(q, kc, v)


def _band_block(qa, qb, k_span, v_span, rel):
    keep = (rel >= -HALF_WINDOW) & (rel <= HALF_WINDOW)
    o_heads, lse_heads = [], []
    for qh in (qa, qb):
        s = lax.dot_general(qh, k_span, (((1,), (1,)), ((), ())), preferred_element_type=F32)
        s = jnp.where(keep, s, MASK_VALUE)
        m = jnp.max(s, axis=-1, keepdims=True)
        p = jnp.exp(s - m)
        l = jnp.sum(p, axis=-1, keepdims=True)
        o_heads.append(jnp.dot(p.astype(BF16), v_span, preferred_element_type=F32) / l)
        lse_heads.append(m + jnp.log(l))
    lane = lax.broadcasted_iota(jnp.int32, o_heads[0].shape, 1)
    first = lane < HEAD_DIM
    o = jnp.where(first, o_heads[0], o_heads[1])
    lse = jnp.where(first, lse_heads[0], lse_heads[1])
    return o, lse


def _dil_kernel(q_ref, k_ref, v_ref, o_ref, ob_ref, lse_ref):
    s_len = q_ref.shape[0]
    lane = lax.broadcasted_iota(jnp.int32, (1, PAIR), 1)
    first = lane < HEAD_DIM
    for n, (_, dil) in enumerate(DIL_PAIRS):
        sub_len = s_len // dil
        span = min(Q_BLOCK + 2 * HALF_WINDOW, sub_len)
        row = lax.broadcasted_iota(jnp.int32, (Q_BLOCK, span), 0)
        col = lax.broadcasted_iota(jnp.int32, (Q_BLOCK, span), 1)
        for r in range(dil):
            def sub(ref):
                if dil == 1:
                    return ref[...]
                return ref[pl.ds(r, sub_len, stride=dil), :]
            qs = sub(q_ref)
            qa = jnp.where(first, qs, 0.0).astype(BF16)
            qb = jnp.where(first, 0.0, qs).astype(BF16)
            ks = sub(k_ref).astype(BF16)
            vs = sub(v_ref).astype(BF16)
            for j in range(sub_len // Q_BLOCK):
                q0 = j * Q_BLOCK
                k0 = min(max(q0 - HALF_WINDOW, 0), sub_len - span)
                rel = col - row + (k0 - q0)
                o, lse = _band_block(qa[q0:q0 + Q_BLOCK], qb[q0:q0 + Q_BLOCK],
                                     ks[k0:k0 + span], vs[k0:k0 + span], rel)
                if dil == 1:
                    dst = pl.ds(q0, Q_BLOCK)
                else:
                    dst = pl.ds(r + dil * q0, Q_BLOCK, stride=dil)
                ob_ref[n, dst, :] = o
                lse_ref[n, dst, :] = lse
    lses = [lse_ref[n] for n in range(len(DIL_PAIRS))]
    top = functools.reduce(jnp.maximum, lses)
    ws = [jnp.exp(l - top) for l in lses]
    den = functools.reduce(lambda a, c: a + c, ws)
    num = functools.reduce(lambda a, c: a + c, [w * ob_ref[n] for n, w in enumerate(ws)])
    o_ref[...] = (num / den).astype(o_ref.dtype)


def _dilated(qd, kd, vd):
    b, s, _ = qd.shape
    blk = pl.BlockSpec((None, s, PAIR), lambda bi, p: (bi, 0, p))
    return pl.pallas_call(
        _dil_kernel,
        grid=(b, N_DIL_HEADS // 2),
        in_specs=[blk, blk, blk],
        out_specs=blk,
        out_shape=jax.ShapeDtypeStruct((b, s, D_DIL_OUT), BF16),
        scratch_shapes=[pltpu.VMEM((len(DIL_PAIRS), s, PAIR), F32),
                        pltpu.VMEM((len(DIL_PAIRS), s, PAIR), F32)],
        compiler_params=pltpu.CompilerParams(
            dimension_semantics=("parallel", "parallel"),
            vmem_limit_bytes=V7X_VMEM_LIMIT_BYTES),
        name="dilated",
    )(qd, kd, vd)


def _rope_tables(seq, rot_dim):
    half = rot_dim // 2
    inv_freq = ROPE_THETA ** (-jnp.arange(half, dtype=F32) * (2.0 / rot_dim))
    ang = jnp.arange(seq, dtype=F32)[:, None] * inv_freq[None, :]
    return jnp.cos(ang), jnp.sin(ang)


def _rope_lane_tables(seq):
    cos_m, sin_m = _rope_tables(seq, MLA_ROPE_DIM)
    cos_p, sin_p = _rope_tables(seq, PARTIAL_ROT_DIM)
    zm, zp = jnp.zeros_like(sin_m), jnp.zeros_like(sin_p)

    def mla(first, second, fill):
        unit = jnp.concatenate([first, second], axis=-1)
        pad = jnp.full((seq, LANES - 2 * MLA_ROPE_DIM), fill, F32)
        return jnp.concatenate([unit, unit, pad], axis=-1)

    def dil(first, second, fill):
        rest = jnp.full((seq, HEAD_DIM - PARTIAL_ROT_DIM), fill, F32)
        head = jnp.concatenate([first, second, rest], axis=-1)
        return jnp.concatenate([head, head], axis=-1)

    return jnp.concatenate([
        mla(cos_m, cos_m, 1.0), mla(-sin_m, zm, 0.0), mla(zm, sin_m, 0.0),
        dil(cos_p, cos_p, 1.0), dil(-sin_p, zp, 0.0), dil(zp, sin_p, 0.0)], axis=-1)


def _reorder_w_in(w_in):
    c_q = w_in[:, :MLA_Q_LORA]
    c_kv = w_in[:, MLA_Q_LORA:MLA_Q_LORA + MLA_KV_LORA]
    k_r = w_in[:, MLA_Q_LORA + MLA_KV_LORA:MLA_Q_LORA + MLA_KV_LORA + MLA_ROPE_DIM]
    rest = w_in[:, MLA_Q_LORA + MLA_KV_LORA + MLA_ROPE_DIM:]
    pad = jnp.zeros((w_in.shape[0], LANES - 2 * MLA_ROPE_DIM), w_in.dtype)
    return jnp.concatenate([c_q, c_kv, k_r, k_r, pad, rest], axis=-1)


def _reorder_w_uq(w_uq):
    w = w_uq.reshape(MLA_Q_LORA, N_PAIRS, 2, MLA_NOPE_DIM + MLA_ROPE_DIM)
    nope = w[..., :MLA_NOPE_DIM].reshape(MLA_Q_LORA, N_PAIRS, 2 * MLA_NOPE_DIM)
    rope = w[..., MLA_NOPE_DIM:].reshape(MLA_Q_LORA, N_PAIRS, 2 * MLA_ROPE_DIM)
    pad = jnp.zeros((MLA_Q_LORA, N_PAIRS, LANES - 2 * MLA_ROPE_DIM), w_uq.dtype)
    return jnp.concatenate([nope, rope, pad], axis=-1).reshape(MLA_Q_LORA, N_PAIRS * MLA_PAIR_W)


def _reorder_w_ukv(w_ukv):
    w = w_ukv.reshape(MLA_KV_LORA, N_MLA_HEADS, MLA_NOPE_DIM + MLA_V_DIM)
    k_nope = w[..., :MLA_NOPE_DIM].reshape(MLA_KV_LORA, D_MLA_OUT)
    v = w[..., MLA_NOPE_DIM:].reshape(MLA_KV_LORA, D_MLA_OUT)
    return jnp.concatenate([k_nope, v], axis=-1)


def kernel(x, ffn1_norm, ffn1_w_gate, ffn1_w_up, ffn1_w_down, mix_norm, w_in, mla_q_norm, mla_w_uq, mla_kv_norm, mla_w_ukv, mla_out_norm, dil_out_norm, w_out, ffn2_norm, ffn2_w_gate, ffn2_w_up, ffn2_w_down, final_norm):
    b, s, d = x.shape
    assert ffn1_norm.shape[0] == 1, "single-layer block: the final norm is fused into its last FFN"
    assert d == D_MODEL and s % MLA_Q_ROWS == 0 and s % (Q_BLOCK * DIL_PAIRS[-1][1]) == 0
    tab = _rope_lane_tables(s)
    x1 = _ffn1(x.reshape(b * s, d), ffn1_norm[0][None], ffn1_w_gate[0].astype(BF16),
               ffn1_w_up[0].astype(BF16), ffn1_w_down[0].astype(BF16))
    q, kc, v, qd, kd, vd = _proj(
        x1.reshape(b, s, d), mix_norm[0][None], _reorder_w_in(w_in[0]).astype(BF16),
        mla_q_norm[0][None], _reorder_w_uq(mla_w_uq[0]).astype(BF16),
        mla_kv_norm[0][None], _reorder_w_ukv(mla_w_ukv[0]).astype(BF16), tab)
    o_mla = _mla(q, kc, v).reshape(b * s, D_MLA_OUT)
    o_dil = _dilated(qd, kd, vd).reshape(b * s, D_DIL_OUT)
    wo = w_out[0].astype(BF16)
    y = _ffn2(x1, o_mla, o_dil, mla_out_norm[0][None], dil_out_norm[0][None],
              wo[:D_MLA_OUT], wo[D_MLA_OUT:], ffn2_norm[0][None],
              ffn2_w_gate[0].astype(BF16), ffn2_w_up[0].astype(BF16),
              ffn2_w_down[0].astype(BF16), final_norm[None])
    return y.reshape(b, s, d)
```

```python
import functools

import numpy as np
import jax
import jax.numpy as jnp
from jax import lax
from jax.experimental import pallas as pl
from jax.experimental.pallas import tpu as pltpu

F32 = jnp.float32
BF16 = jnp.bfloat16

D_MODEL = 1024
HEAD_DIM = 64
N_MLA_HEADS = 8
N_DIL_HEADS = 8
MLA_NOPE_DIM = 64
MLA_ROPE_DIM = 32
MLA_V_DIM = 64
MLA_Q_LORA = 384
MLA_KV_LORA = 256
D_MLA_OUT = N_MLA_HEADS * MLA_V_DIM
D_DIL_OUT = N_DIL_HEADS * HEAD_DIM
DIL_PAIRS = ((128, 1), (512, 4), (2048, 16))
PARTIAL_ROT_DIM = 16
ROPE_THETA = 500000.0
D_FF = 2816
Q_BLOCK = 128
NORM_EPS = 1e-6
MASK_VALUE = -1e30
HALF_WINDOW = 64
assert all(w // (2 * d) == HALF_WINDOW for w, d in DIL_PAIRS)
LOG2_E = 1.4426950408889634

LANES = 128
V7X_VMEM_LIMIT_BYTES = 56 * 1024 * 1024

FFN_ROWS = 512
PROJ_ROWS = 512
MLA_Q_ROWS = 512
PAIR = 2 * HEAD_DIM
N_PAIRS = N_MLA_HEADS // 2
MLA_PAIR_W = 2 * LANES

_OFF_CQ = 0
_OFF_CKV = _OFF_CQ + MLA_Q_LORA
_OFF_KR = _OFF_CKV + MLA_KV_LORA
_OFF_QD = _OFF_KR + LANES
_OFF_KD = _OFF_QD + D_DIL_OUT
_OFF_VD = _OFF_KD + D_DIL_OUT
_PROJ_W = _OFF_VD + D_DIL_OUT


def _rms(x, g):
    ms = jnp.mean(x * x, axis=-1, keepdims=True)
    return x * lax.rsqrt(ms + NORM_EPS) * g


def _rope(x, c, sa, sb, half):
    return x * c + pltpu.roll(x, LANES - half, 1) * sa + pltpu.roll(x, half, 1) * sb


def _swiglu_residual(x, g_ref, wg_ref, wu_ref, wd_ref):
    h = _rms(x, g_ref[...]).astype(BF16)
    gate = jnp.dot(h, wg_ref[...], preferred_element_type=F32)
    up = jnp.dot(h, wu_ref[...], preferred_element_type=F32)
    act = (gate * jax.nn.sigmoid(gate) * up).astype(BF16)
    return x + 0.5 * jnp.dot(act, wd_ref[...], preferred_element_type=F32)


def _ffn1_kernel(x_ref, g_ref, wg_ref, wu_ref, wd_ref, o_ref):
    o_ref[...] = _swiglu_residual(x_ref[...], g_ref, wg_ref, wu_ref, wd_ref)


def _ffn2_kernel(x_ref, om_ref, od_ref, gm_ref, gd_ref, woa_ref, wob_ref,
                 g_ref, wg_ref, wu_ref, wd_ref, gf_ref, o_ref):
    nm = _rms(om_ref[...].astype(F32), gm_ref[...]).astype(BF16)
    nd = _rms(od_ref[...].astype(F32), gd_ref[...]).astype(BF16)
    x = (x_ref[...]
         + jnp.dot(nm, woa_ref[...], preferred_element_type=F32)
         + jnp.dot(nd, wob_ref[...], preferred_element_type=F32))
    o_ref[...] = _rms(_swiglu_residual(x, g_ref, wg_ref, wu_ref, wd_ref), gf_ref[...])


def _resident(shape):
    return pl.BlockSpec(shape, lambda *_: (0,) * len(shape), pipeline_mode=pl.Buffered(1))


def _ffn1(x2d, g, wg, wu, wd):
    n = x2d.shape[0]
    row = pl.BlockSpec((FFN_ROWS, D_MODEL), lambda i: (i, 0))
    return pl.pallas_call(
        _ffn1_kernel,
        grid=(n // FFN_ROWS,),
        in_specs=[row, _resident((1, D_MODEL)), _resident((D_MODEL, D_FF)),
                  _resident((D_MODEL, D_FF)), _resident((D_FF, D_MODEL))],
        out_specs=row,
        out_shape=jax.ShapeDtypeStruct((n, D_MODEL), F32),
        compiler_params=pltpu.CompilerParams(
            dimension_semantics=("parallel",), vmem_limit_bytes=V7X_VMEM_LIMIT_BYTES),
        name="ffn1",
    )(x2d, g, wg, wu, wd)


def _ffn2(x2d, om, od, gm, gd, woa, wob, g, wg, wu, wd, gf):
    n = x2d.shape[0]
    row = pl.BlockSpec((FFN_ROWS, D_MODEL), lambda i: (i, 0))
    half_row = pl.BlockSpec((FFN_ROWS, D_MLA_OUT), lambda i: (i, 0))
    return pl.pallas_call(
        _ffn2_kernel,
        grid=(n // FFN_ROWS,),
        in_specs=[row, half_row, half_row,
                  _resident((1, D_MLA_OUT)), _resident((1, D_DIL_OUT)),
                  _resident((D_MLA_OUT, D_MODEL)), _resident((D_DIL_OUT, D_MODEL)),
                  _resident((1, D_MODEL)), _resident((D_MODEL, D_FF)),
                  _resident((D_MODEL, D_FF)), _resident((D_FF, D_MODEL)),
                  _resident((1, D_MODEL))],
        out_specs=row,
        out_shape=jax.ShapeDtypeStruct((n, D_MODEL), F32),
        compiler_params=pltpu.CompilerParams(
            dimension_semantics=("parallel",), vmem_limit_bytes=V7X_VMEM_LIMIT_BYTES),
        name="ffn2",
    )(x2d, om, od, gm, gd, woa, wob, g, wg, wu, wd, gf)


def _proj_kernel(x_ref, g_ref, win_ref, gq_ref, wuq_ref, gkv_ref, wukv_ref, tab_ref,
                 q_ref, kc_ref, v_ref, qd_ref, kd_ref, vd_ref):
    h = _rms(x_ref[...], g_ref[...]).astype(BF16)
    proj = jnp.dot(h, win_ref[...], preferred_element_type=F32)

    cm, sam, sbm = (tab_ref[:, i * LANES:(i + 1) * LANES] for i in range(3))
    cd, sad, sbd = (tab_ref[:, i * LANES:(i + 1) * LANES] for i in range(3, 6))
    rope_m = functools.partial(_rope, c=cm, sa=sam, sb=sbm, half=MLA_ROPE_DIM // 2)
    rope_d = functools.partial(_rope, c=cd, sa=sad, sb=sbd, half=PARTIAL_ROT_DIM // 2)

    cq = _rms(proj[:, _OFF_CQ:_OFF_CQ + MLA_Q_LORA], gq_ref[...]).astype(BF16)
    q = jnp.dot(cq, wuq_ref[...], preferred_element_type=F32)
    q_scale = LOG2_E * (MLA_NOPE_DIM + MLA_ROPE_DIM) ** -0.5
    for p in range(N_PAIRS):
        lo = p * MLA_PAIR_W
        q_ref[:, lo:lo + LANES] = (q[:, lo:lo + LANES] * q_scale).astype(BF16)
        q_ref[:, lo + LANES:lo + 2 * LANES] = (
            rope_m(q[:, lo + LANES:lo + 2 * LANES]) * q_scale).astype(BF16)

    ckv = _rms(proj[:, _OFF_CKV:_OFF_CKV + MLA_KV_LORA], gkv_ref[...]).astype(BF16)
    kv = jnp.dot(ckv, wukv_ref[...], preferred_element_type=F32)
    k_rope = rope_m(proj[:, _OFF_KR:_OFF_KR + LANES]).astype(BF16)
    for p in range(N_PAIRS):
        lo = p * MLA_PAIR_W
        kc_ref[:, lo:lo + LANES] = kv[:, p * PAIR:(p + 1) * PAIR].astype(BF16)
        kc_ref[:, lo + LANES:lo + 2 * LANES] = k_rope
    v_ref[...] = kv[:, D_MLA_OUT:].astype(BF16)

    d_scale = LOG2_E * HEAD_DIM ** -0.5
    for p in range(N_DIL_HEADS // 2):
        lo = p * PAIR
        qd_ref[:, lo:lo + PAIR] = rope_d(proj[:, _OFF_QD + lo:_OFF_QD + lo + PAIR]) * d_scale
        kd_ref[:, lo:lo + PAIR] = rope_d(proj[:, _OFF_KD + lo:_OFF_KD + lo + PAIR])
    vd_ref[...] = proj[:, _OFF_VD:_OFF_VD + D_DIL_OUT]


def _proj(x3d, g, win, gq, wuq, gkv, wukv, tab):
    b, s, _ = x3d.shape

    def tok(width):
        return pl.BlockSpec((None, PROJ_ROWS, width), lambda si, bi: (bi, si, 0))

    def out(width, dtype):
        return jax.ShapeDtypeStruct((b, s, width), dtype)

    return pl.pallas_call(
        _proj_kernel,
        grid=(s // PROJ_ROWS, b),
        in_specs=[tok(D_MODEL), _resident((1, D_MODEL)), _resident((D_MODEL, _PROJ_W)),
                  _resident((1, MLA_Q_LORA)), _resident((MLA_Q_LORA, N_PAIRS * MLA_PAIR_W)),
                  _resident((1, MLA_KV_LORA)), _resident((MLA_KV_LORA, 2 * D_MLA_OUT)),
                  pl.BlockSpec((PROJ_ROWS, 6 * LANES), lambda si, bi: (si, 0))],
        out_specs=[tok(N_PAIRS * MLA_PAIR_W), tok(N_PAIRS * MLA_PAIR_W), tok(D_MLA_OUT),
                   tok(D_DIL_OUT), tok(D_DIL_OUT), tok(D_DIL_OUT)],
        out_shape=[out(N_PAIRS * MLA_PAIR_W, BF16), out(N_PAIRS * MLA_PAIR_W, BF16),
                   out(D_MLA_OUT, BF16), out(D_DIL_OUT, F32), out(D_DIL_OUT, F32),
                   out(D_DIL_OUT, F32)],
        compiler_params=pltpu.CompilerParams(
            dimension_semantics=("parallel", "parallel"),
            vmem_limit_bytes=V7X_VMEM_LIMIT_BYTES),
        name="proj",
    )(x3d, g, win, gq, wuq, gkv, wukv, tab)


def _mla_kernel(q_ref, kc_ref, v_ref, o_ref):
    kc = kc_ref[...]
    v = v_ref[...]
    first_v = lax.broadcasted_iota(jnp.int32, v.shape, 1) < MLA_V_DIM
    one = jnp.ones_like(v)
    v_heads = (jnp.where(first_v, v, one), jnp.where(first_v, one, v))
    lane_q = lax.broadcasted_iota(jnp.int32, (MLA_Q_ROWS, MLA_PAIR_W), 1)
    first_o = lax.broadcasted_iota(jnp.int32, (MLA_Q_ROWS, PAIR), 1) < MLA_V_DIM
    for c in range(q_ref.shape[0] // MLA_Q_ROWS):
        rows = slice(c * MLA_Q_ROWS, (c + 1) * MLA_Q_ROWS)
        q = q_ref[rows, :]
        res = []
        for h in range(2):
            nope = (lane_q >= h * MLA_NOPE_DIM) & (lane_q < (h + 1) * MLA_NOPE_DIM)
            rope = ((lane_q >= LANES + h * MLA_ROPE_DIM)
                    & (lane_q < LANES + (h + 1) * MLA_ROPE_DIM))
            qh = jnp.where(nope | rope, q, jnp.zeros_like(q))
            s = lax.dot_general(qh, kc, (((1,), (1,)), ((), ())), preferred_element_type=F32)
            m = jnp.max(s, axis=-1, keepdims=True)
            p = jnp.exp2((s - m).astype(BF16))
            res.append(jnp.dot(p, v_heads[h], preferred_element_type=F32))
        num = jnp.where(first_o, res[0], res[1])
        den = pltpu.roll(jnp.where(first_o, res[1], res[0]), MLA_V_DIM, 1)
        o_ref[rows, :] = (num / den).astype(o_ref.dtype)


def _mla(q, kc, v):
    b, s, _ = q.shape
    return pl.pallas_call(
        _mla_kernel,
        grid=(b, N_PAIRS),
        in_specs=[pl.BlockSpec((None, s, MLA_PAIR_W), lambda bi, p: (bi, 0, p)),
                  pl.BlockSpec((None, s, MLA_PAIR_W), lambda bi, p: (bi, 0, p)),
                  pl.BlockSpec((None, s, PAIR), lambda bi, p: (bi, 0, p))],
        out_specs=pl.BlockSpec((None, s, PAIR), lambda bi, p: (bi, 0, p)),
        out_shape=jax.ShapeDtypeStruct((b, s, D_MLA_OUT), BF16),
        compiler_params=pltpu.CompilerParams(
            dimension_semantics=("parallel", "parallel"),
            vmem_limit_bytes=V7X_VMEM_LIMIT_BYTES),
        name="mla",
    )(q, kc, v)


DIL_SPAN = Q_BLOCK + 2 * HALF_WINDOW
DIL_MID = DIL_PAIRS[1][1]
assert DIL_PAIRS[0][1] == 1 and DIL_PAIRS[2][1] == DIL_MID * DIL_MID


def _band_bias():
    row = np.arange(Q_BLOCK)[:, None]
    col = np.arange(DIL_SPAN)[None, :]
    kinds = [np.abs(col - row - off) <= HALF_WINDOW
             for off in (0, HALF_WINDOW, 2 * HALF_WINDOW)]
    kinds.append((np.abs(col - row) <= HALF_WINDOW) & (col < Q_BLOCK))
    bias = np.where(np.stack(kinds), 0.0, MASK_VALUE).astype(np.float32)
    return np.concatenate([bias, bias], axis=1)


def _band_block(q_stack, k_span, v_ext, bias):
    s = lax.dot_general(q_stack, k_span, (((1,), (1,)), ((), ())),
                        preferred_element_type=F32) + bias
    m = jnp.max(s, axis=-1, keepdims=True)
    p = jnp.exp2((s - m).astype(BF16))
    res = jnp.dot(p, v_ext, preferred_element_type=F32)
    first = lax.broadcasted_iota(jnp.int32, (Q_BLOCK, PAIR), 1) < HEAD_DIM
    acc = jnp.where(first, res[:Q_BLOCK, :PAIR], res[Q_BLOCK:, :PAIR])
    l = jnp.where(first, res[:Q_BLOCK, PAIR:], res[Q_BLOCK:, PAIR:])
    m_lanes = jnp.where(first, jnp.broadcast_to(m[:Q_BLOCK], (Q_BLOCK, PAIR)),
                        jnp.broadcast_to(m[Q_BLOCK:], (Q_BLOCK, PAIR)))
    return acc, l, m_lanes


def _dil_kernel(q_ref, k_ref, v_ref, bias_ref, o_ref, mid_in_ref, nat_ref, mid_ref):
    s_len = q_ref.shape[0]
    mid_len = s_len // DIL_MID
    far_len = mid_len // DIL_MID
    first = lax.broadcasted_iota(jnp.int32, (1, PAIR), 1) < HEAD_DIM

    def run(q, k, v, sub_len, write):
        qa = jnp.where(first, q, 0.0).astype(BF16)
        qb = jnp.where(first, 0.0, q).astype(BF16)
        kb = k.astype(BF16)
        vb = v.astype(BF16)
        v_ext = jnp.concatenate([vb, jnp.ones_like(vb)], axis=1)
        span = min(DIL_SPAN, sub_len)
        for j in range(sub_len // Q_BLOCK):
            q0 = j * Q_BLOCK
            k0 = min(max(q0 - HALF_WINDOW, 0), sub_len - span)
            kind = 3 if span < DIL_SPAN else (q0 - k0) // HALF_WINDOW
            q_stack = jnp.concatenate([qa[q0:q0 + Q_BLOCK], qb[q0:q0 + Q_BLOCK]], axis=0)
            write(j, _band_block(q_stack, kb[k0:k0 + span], v_ext[k0:k0 + span],
                                 bias_ref[kind][:, :span]))

    def write_near(j, vals):
        for t, val in enumerate(vals):
            nat_ref[t, j * Q_BLOCK:(j + 1) * Q_BLOCK, :] = val
    run(q_ref[...], k_ref[...], v_ref[...], s_len, write_near)

    for t, ref in enumerate((q_ref, k_ref, v_ref)):
        for r in range(DIL_MID):
            mid_in_ref[t, r] = ref[pl.ds(r, mid_len, stride=DIL_MID), :]

    for r in range(DIL_MID):
        def write_mid(j, vals, r=r):
            for t, val in enumerate(vals):
                mid_ref[0, t, r, j * Q_BLOCK:(j + 1) * Q_BLOCK, :] = val
        run(mid_in_ref[0, r], mid_in_ref[1, r], mid_in_ref[2, r], mid_len, write_mid)

    for r in range(DIL_MID):
        for rr in range(DIL_MID):
            sl = pl.ds(rr, far_len, stride=DIL_MID)
            def write_far(j, vals, r=r, sl=sl):
                for t, val in enumerate(vals):
                    mid_ref[1, t, r, sl, :] = val
            run(mid_in_ref[0, r, sl, :], mid_in_ref[1, r, sl, :], mid_in_ref[2, r, sl, :],
                far_len, write_far)

    for r in range(DIL_MID):
        nat = pl.ds(r, mid_len, stride=DIL_MID)
        accs = (nat_ref[0, nat, :], mid_ref[0, 0, r], mid_ref[1, 0, r])
        ls = (nat_ref[1, nat, :], mid_ref[0, 1, r], mid_ref[1, 1, r])
        ms = (nat_ref[2, nat, :], mid_ref[0, 2, r], mid_ref[1, 2, r])
        top = jnp.maximum(jnp.maximum(ms[0], ms[1]), ms[2])
        es = [jnp.exp2(m - top) for m in ms]
        num = es[0] * accs[0] + es[1] * accs[1] + es[2] * accs[2]
        den = es[0] * ls[0] + es[1] * ls[1] + es[2] * ls[2]
        o_ref[nat, :] = num / den


def _dilated(qd, kd, vd):
    b, s, _ = qd.shape
    blk = pl.BlockSpec((None, s, PAIR), lambda bi, p: (bi, 0, p))
    bias = jnp.asarray(_band_bias())
    return pl.pallas_call(
        _dil_kernel,
        grid=(b, N_DIL_HEADS // 2),
        in_specs=[blk, blk, blk, _resident(bias.shape)],
        out_specs=blk,
        out_shape=jax.ShapeDtypeStruct((b, s, D_DIL_OUT), F32),
        scratch_shapes=[pltpu.VMEM((3, DIL_MID, s // DIL_MID, PAIR), F32),
                        pltpu.VMEM((3, s, PAIR), F32),
                        pltpu.VMEM((2, 3, DIL_MID, s // DIL_MID, PAIR), F32)],
        compiler_params=pltpu.CompilerParams(
            dimension_semantics=("parallel", "parallel"),
            vmem_limit_bytes=V7X_VMEM_LIMIT_BYTES),
        name="dilated",
    )(qd, kd, vd, bias)


def _rope_tables(seq, rot_dim):
    half = rot_dim // 2
    inv_freq = ROPE_THETA ** (-jnp.arange(half, dtype=F32) * (2.0 / rot_dim))
    ang = jnp.arange(seq, dtype=F32)[:, None] * inv_freq[None, :]
    return jnp.cos(ang), jnp.sin(ang)


def _rope_lane_tables(seq):
    cos_m, sin_m = _rope_tables(seq, MLA_ROPE_DIM)
    cos_p, sin_p = _rope_tables(seq, PARTIAL_ROT_DIM)
    zm, zp = jnp.zeros_like(sin_m), jnp.zeros_like(sin_p)

    def mla(first, second, fill):
        unit = jnp.concatenate([first, second], axis=-1)
        pad = jnp.full((seq, LANES - 2 * MLA_ROPE_DIM), fill, F32)
        return jnp.concatenate([unit, unit, pad], axis=-1)

    def dil(first, second, fill):
        rest = jnp.full((seq, HEAD_DIM - PARTIAL_ROT_DIM), fill, F32)
        head = jnp.concatenate([first, second, rest], axis=-1)
        return jnp.concatenate([head, head], axis=-1)

    return jnp.concatenate([
        mla(cos_m, cos_m, 1.0), mla(-sin_m, zm, 0.0), mla(zm, sin_m, 0.0),
        dil(cos_p, cos_p, 1.0), dil(-sin_p, zp, 0.0), dil(zp, sin_p, 0.0)], axis=-1)


def _reorder_w_in(w_in):
    c_q = w_in[:, :MLA_Q_LORA]
    c_kv = w_in[:, MLA_Q_LORA:MLA_Q_LORA + MLA_KV_LORA]
    k_r = w_in[:, MLA_Q_LORA + MLA_KV_LORA:MLA_Q_LORA + MLA_KV_LORA + MLA_ROPE_DIM]
    rest = w_in[:, MLA_Q_LORA + MLA_KV_LORA + MLA_ROPE_DIM:]
    pad = jnp.zeros((w_in.shape[0], LANES - 2 * MLA_ROPE_DIM), w_in.dtype)
    return jnp.concatenate([c_q, c_kv, k_r, k_r, pad, rest], axis=-1)


def _reorder_w_uq(w_uq):
    w = w_uq.reshape(MLA_Q_LORA, N_PAIRS, 2, MLA_NOPE_DIM + MLA_ROPE_DIM)
    nope = w[..., :MLA_NOPE_DIM].reshape(MLA_Q_LORA, N_PAIRS, 2 * MLA_NOPE_DIM)
    rope = w[..., MLA_NOPE_DIM:].reshape(MLA_Q_LORA, N_PAIRS, 2 * MLA_ROPE_DIM)
    pad = jnp.zeros((MLA_Q_LORA, N_PAIRS, LANES - 2 * MLA_ROPE_DIM), w_uq.dtype)
    return jnp.concatenate([nope, rope, pad], axis=-1).reshape(MLA_Q_LORA, N_PAIRS * MLA_PAIR_W)


def _reorder_w_ukv(w_ukv):
    w = w_ukv.reshape(MLA_KV_LORA, N_MLA_HEADS, MLA_NOPE_DIM + MLA_V_DIM)
    k_nope = w[..., :MLA_NOPE_DIM].reshape(MLA_KV_LORA, D_MLA_OUT)
    v = w[..., MLA_NOPE_DIM:].reshape(MLA_KV_LORA, D_MLA_OUT)
    return jnp.concatenate([k_nope, v], axis=-1)


def kernel(x, ffn1_norm, ffn1_w_gate, ffn1_w_up, ffn1_w_down, mix_norm, w_in, mla_q_norm, mla_w_uq, mla_kv_norm, mla_w_ukv, mla_out_norm, dil_out_norm, w_out, ffn2_norm, ffn2_w_gate, ffn2_w_up, ffn2_w_down, final_norm):
    b, s, d = x.shape
    assert ffn1_norm.shape[0] == 1, "single-layer block: the final norm is fused into its last FFN"
    assert d == D_MODEL and s % MLA_Q_ROWS == 0 and s % (Q_BLOCK * DIL_PAIRS[-1][1]) == 0
    tab = _rope_lane_tables(s)
    x1 = _ffn1(x.reshape(b * s, d), ffn1_norm[0][None], ffn1_w_gate[0].astype(BF16),
               ffn1_w_up[0].astype(BF16), ffn1_w_down[0].astype(BF16))
    q, kc, v, qd, kd, vd = _proj(
        x1.reshape(b, s, d), mix_norm[0][None], _reorder_w_in(w_in[0]).astype(BF16),
        mla_q_norm[0][None], _reorder_w_uq(mla_w_uq[0]).astype(BF16),
        mla_kv_norm[0][None], _reorder_w_ukv(mla_w_ukv[0]).astype(BF16), tab)
    o_mla = _mla(q, kc, v).reshape(b * s, D_MLA_OUT)
    o_dil = _dilated(qd, kd, vd).reshape(b * s, D_DIL_OUT)
    wo = w_out[0].astype(BF16)
    y = _ffn2(x1, o_mla, o_dil, mla_out_norm[0][None], dil_out_norm[0][None],
              wo[:D_MLA_OUT], wo[D_MLA_OUT:], ffn2_norm[0][None],
              ffn2_w_gate[0].astype(BF16), ffn2_w_up[0].astype(BF16),
              ffn2_w_down[0].astype(BF16), final_norm[None])
    return y.reshape(b, s, d)
```

```python
import functools

import numpy as np
import jax
import jax.numpy as jnp
from jax import lax
from jax.experimental import pallas as pl
from jax.experimental.pallas import tpu as pltpu

F32 = jnp.float32
BF16 = jnp.bfloat16

D_MODEL = 1024
HEAD_DIM = 64
N_MLA_HEADS = 8
N_DIL_HEADS = 8
MLA_NOPE_DIM = 64
MLA_ROPE_DIM = 32
MLA_V_DIM = 64
MLA_Q_LORA = 384
MLA_KV_LORA = 256
D_MLA_OUT = N_MLA_HEADS * MLA_V_DIM
D_DIL_OUT = N_DIL_HEADS * HEAD_DIM
DIL_PAIRS = ((128, 1), (512, 4), (2048, 16))
PARTIAL_ROT_DIM = 16
ROPE_THETA = 500000.0
D_FF = 2816
Q_BLOCK = 128
NORM_EPS = 1e-6
MASK_VALUE = -1e30
HALF_WINDOW = 64
assert all(w // (2 * d) == HALF_WINDOW for w, d in DIL_PAIRS)
LOG2_E = 1.4426950408889634

LANES = 128
V7X_VMEM_LIMIT_BYTES = 56 * 1024 * 1024

FFN_ROWS = 512
PROJ_ROWS = 512
MLA_Q_ROWS = 512
MLA_KEY_TILE = 256
PAIR = 2 * HEAD_DIM
N_PAIRS = N_MLA_HEADS // 2
MLA_PAIR_W = 2 * LANES

_OFF_CQ = 0
_OFF_CKV = _OFF_CQ + MLA_Q_LORA
_OFF_KR = _OFF_CKV + MLA_KV_LORA
_OFF_QD = _OFF_KR + LANES
_OFF_KD = _OFF_QD + D_DIL_OUT
_OFF_VD = _OFF_KD + D_DIL_OUT
_PROJ_W = _OFF_VD + D_DIL_OUT


def _rms(x, g):
    ms = jnp.mean(x * x, axis=-1, keepdims=True)
    return x * lax.rsqrt(ms + NORM_EPS) * g


def _rope(x, c, sa, sb, half):
    return x * c + pltpu.roll(x, LANES - half, 1) * sa + pltpu.roll(x, half, 1) * sb


def _swiglu_residual(x, g_ref, wg_ref, wu_ref, wd_ref):
    h = _rms(x, g_ref[...]).astype(BF16)
    gate = jnp.dot(h, wg_ref[...], preferred_element_type=F32)
    up = jnp.dot(h, wu_ref[...], preferred_element_type=F32)
    act = (gate * jax.nn.sigmoid(gate) * up).astype(BF16)
    return x + 0.5 * jnp.dot(act, wd_ref[...], preferred_element_type=F32)


def _ffn1_kernel(x_ref, g_ref, wg_ref, wu_ref, wd_ref, o_ref):
    o_ref[...] = _swiglu_residual(x_ref[...], g_ref, wg_ref, wu_ref, wd_ref)


def _ffn2_kernel(x_ref, om_ref, od_ref, gm_ref, gd_ref, woa_ref, wob_ref,
                 g_ref, wg_ref, wu_ref, wd_ref, gf_ref, o_ref):
    nm = _rms(om_ref[...].astype(F32), gm_ref[...]).astype(BF16)
    nd = _rms(od_ref[...].astype(F32), gd_ref[...]).astype(BF16)
    x = (x_ref[...]
         + jnp.dot(nm, woa_ref[...], preferred_element_type=F32)
         + jnp.dot(nd, wob_ref[...], preferred_element_type=F32))
    o_ref[...] = _rms(_swiglu_residual(x, g_ref, wg_ref, wu_ref, wd_ref), gf_ref[...])


def _resident(shape):
    return pl.BlockSpec(shape, lambda *_: (0,) * len(shape), pipeline_mode=pl.Buffered(1))


def _ffn1(x2d, g, wg, wu, wd):
    n = x2d.shape[0]
    row = pl.BlockSpec((FFN_ROWS, D_MODEL), lambda i: (i, 0))
    return pl.pallas_call(
        _ffn1_kernel,
        grid=(n // FFN_ROWS,),
        in_specs=[row, _resident((1, D_MODEL)), _resident((D_MODEL, D_FF)),
                  _resident((D_MODEL, D_FF)), _resident((D_FF, D_MODEL))],
        out_specs=row,
        out_shape=jax.ShapeDtypeStruct((n, D_MODEL), F32),
        compiler_params=pltpu.CompilerParams(
            dimension_semantics=("parallel",), vmem_limit_bytes=V7X_VMEM_LIMIT_BYTES),
        name="ffn1",
    )(x2d, g, wg, wu, wd)


def _ffn2(x2d, om, od, gm, gd, woa, wob, g, wg, wu, wd, gf):
    n = x2d.shape[0]
    row = pl.BlockSpec((FFN_ROWS, D_MODEL), lambda i: (i, 0))
    half_row = pl.BlockSpec((FFN_ROWS, D_MLA_OUT), lambda i: (i, 0))
    return pl.pallas_call(
        _ffn2_kernel,
        grid=(n // FFN_ROWS,),
        in_specs=[row, half_row, half_row,
                  _resident((1, D_MLA_OUT)), _resident((1, D_DIL_OUT)),
                  _resident((D_MLA_OUT, D_MODEL)), _resident((D_DIL_OUT, D_MODEL)),
                  _resident((1, D_MODEL)), _resident((D_MODEL, D_FF)),
                  _resident((D_MODEL, D_FF)), _resident((D_FF, D_MODEL)),
                  _resident((1, D_MODEL))],
        out_specs=row,
        out_shape=jax.ShapeDtypeStruct((n, D_MODEL), F32),
        compiler_params=pltpu.CompilerParams(
            dimension_semantics=("parallel",), vmem_limit_bytes=V7X_VMEM_LIMIT_BYTES),
        name="ffn2",
    )(x2d, om, od, gm, gd, woa, wob, g, wg, wu, wd, gf)


def _proj_kernel(x_ref, g_ref, win_ref, gq_ref, wuq_ref, gkv_ref, wuk_ref, wuvt_ref, tab_ref,
                 q_ref, kc_ref, vt_ref, qd_ref, kd_ref, vd_ref):
    h = _rms(x_ref[...], g_ref[...]).astype(BF16)
    proj = jnp.dot(h, win_ref[...], preferred_element_type=F32)

    cm, sam, sbm = (tab_ref[:, i * LANES:(i + 1) * LANES] for i in range(3))
    cd, sad, sbd = (tab_ref[:, i * LANES:(i + 1) * LANES] for i in range(3, 6))
    rope_m = functools.partial(_rope, c=cm, sa=sam, sb=sbm, half=MLA_ROPE_DIM // 2)
    rope_d = functools.partial(_rope, c=cd, sa=sad, sb=sbd, half=PARTIAL_ROT_DIM // 2)

    cq = _rms(proj[:, _OFF_CQ:_OFF_CQ + MLA_Q_LORA], gq_ref[...]).astype(BF16)
    q = jnp.dot(cq, wuq_ref[...], preferred_element_type=F32)
    q_scale = LOG2_E * (MLA_NOPE_DIM + MLA_ROPE_DIM) ** -0.5
    for p in range(N_PAIRS):
        lo = p * MLA_PAIR_W
        q_ref[:, lo:lo + LANES] = (q[:, lo:lo + LANES] * q_scale).astype(BF16)
        q_ref[:, lo + LANES:lo + 2 * LANES] = (
            rope_m(q[:, lo + LANES:lo + 2 * LANES]) * q_scale).astype(BF16)

    ckv = _rms(proj[:, _OFF_CKV:_OFF_CKV + MLA_KV_LORA], gkv_ref[...]).astype(BF16)
    k_nope = jnp.dot(ckv, wuk_ref[...], preferred_element_type=F32)
    k_rope = rope_m(proj[:, _OFF_KR:_OFF_KR + LANES]).astype(BF16)
    for p in range(N_PAIRS):
        lo = p * MLA_PAIR_W
        kc_ref[:, lo:lo + LANES] = k_nope[:, p * PAIR:(p + 1) * PAIR].astype(BF16)
        kc_ref[:, lo + LANES:lo + 2 * LANES] = k_rope
    vt_ref[...] = lax.dot_general(wuvt_ref[...], ckv, (((1,), (1,)), ((), ())),
                                  preferred_element_type=F32).astype(BF16)

    d_scale = LOG2_E * HEAD_DIM ** -0.5
    for p in range(N_DIL_HEADS // 2):
        lo = p * PAIR
        qd_ref[:, lo:lo + PAIR] = rope_d(proj[:, _OFF_QD + lo:_OFF_QD + lo + PAIR]) * d_scale
        kd_ref[:, lo:lo + PAIR] = rope_d(proj[:, _OFF_KD + lo:_OFF_KD + lo + PAIR])
    vd_ref[...] = proj[:, _OFF_VD:_OFF_VD + D_DIL_OUT]


def _proj(x3d, g, win, gq, wuq, gkv, wuk, wuvt, tab):
    b, s, _ = x3d.shape

    def tok(width):
        return pl.BlockSpec((None, PROJ_ROWS, width), lambda si, bi: (bi, si, 0))

    def out(width, dtype):
        return jax.ShapeDtypeStruct((b, s, width), dtype)

    return pl.pallas_call(
        _proj_kernel,
        grid=(s // PROJ_ROWS, b),
        in_specs=[tok(D_MODEL), _resident((1, D_MODEL)), _resident((D_MODEL, _PROJ_W)),
                  _resident((1, MLA_Q_LORA)), _resident((MLA_Q_LORA, N_PAIRS * MLA_PAIR_W)),
                  _resident((1, MLA_KV_LORA)), _resident((MLA_KV_LORA, D_MLA_OUT)),
                  _resident((D_MLA_OUT, MLA_KV_LORA)),
                  pl.BlockSpec((PROJ_ROWS, 6 * LANES), lambda si, bi: (si, 0))],
        out_specs=[tok(N_PAIRS * MLA_PAIR_W), tok(N_PAIRS * MLA_PAIR_W),
                   pl.BlockSpec((None, D_MLA_OUT, PROJ_ROWS), lambda si, bi: (bi, 0, si)),
                   tok(D_DIL_OUT), tok(D_DIL_OUT), tok(D_DIL_OUT)],
        out_shape=[out(N_PAIRS * MLA_PAIR_W, BF16), out(N_PAIRS * MLA_PAIR_W, BF16),
                   jax.ShapeDtypeStruct((b, D_MLA_OUT, s), BF16),
                   out(D_DIL_OUT, F32), out(D_DIL_OUT, F32), out(D_DIL_OUT, F32)],
        compiler_params=pltpu.CompilerParams(
            dimension_semantics=("parallel", "parallel"),
            vmem_limit_bytes=V7X_VMEM_LIMIT_BYTES),
        name="proj",
    )(x3d, g, win, gq, wuq, gkv, wuk, wuvt, tab)


def _mla_kernel(q_ref, kc_ref, vt_ref, o_ref, s_ref):
    s_len = kc_ref.shape[0]
    n_tiles = s_len // MLA_KEY_TILE
    vt = vt_ref[...]
    top = lax.broadcasted_iota(jnp.int32, vt.shape, 0) < MLA_V_DIM
    one = jnp.ones_like(vt)
    v_heads = (jnp.where(top, vt, one), jnp.where(top, one, vt))
    lane_q = lax.broadcasted_iota(jnp.int32, (MLA_Q_ROWS, MLA_PAIR_W), 1)
    chains = [(c, h) for c in range(s_len // MLA_Q_ROWS) for h in range(2)]

    def masked_q(c, h):
        q = q_ref[c * MLA_Q_ROWS:(c + 1) * MLA_Q_ROWS, :]
        nope = (lane_q >= h * MLA_NOPE_DIM) & (lane_q < (h + 1) * MLA_NOPE_DIM)
        rope = ((lane_q >= LANES + h * MLA_ROPE_DIM)
                & (lane_q < LANES + (h + 1) * MLA_ROPE_DIM))
        return jnp.where(nope | rope, q, jnp.zeros_like(q))

    def score_tile(slot, qh, t, run_max):
        keys = slice(t * MLA_KEY_TILE, (t + 1) * MLA_KEY_TILE)
        s_t = lax.dot_general(qh, kc_ref[keys, :], (((1,), (1,)), ((), ())),
                              preferred_element_type=F32)
        s_ref[slot, :, keys] = s_t
        tile_max = functools.reduce(
            jnp.maximum, [s_t[:, i:i + LANES] for i in range(0, MLA_KEY_TILE, LANES)])
        return tile_max if run_max is None else jnp.maximum(run_max, tile_max)

    def pv_tile(slot, h, t, m, acc):
        keys = slice(t * MLA_KEY_TILE, (t + 1) * MLA_KEY_TILE)
        p_t = jnp.exp2((s_ref[slot, :, keys] - m).astype(BF16))
        d = lax.dot_general(v_heads[h][:, keys], p_t, (((1,), (1,)), ((), ())),
                            preferred_element_type=F32)
        return d if acc is None else acc + d

    qh = masked_q(*chains[0])
    run_max = None
    for t in range(n_tiles):
        run_max = score_tile(0, qh, t, run_max)
    m = jnp.max(run_max, axis=-1, keepdims=True)
    res = []
    for i, (c, h) in enumerate(chains):
        nxt = chains[i + 1] if i + 1 < len(chains) else None
        if nxt is not None:
            qh = masked_q(*nxt)
        run_max, acc = None, None
        for t in range(n_tiles):
            if nxt is not None:
                run_max = score_tile((i + 1) % 2, qh, t, run_max)
            acc = pv_tile(i % 2, h, t, m, acc)
        if nxt is not None:
            m = jnp.max(run_max, axis=-1, keepdims=True)
        res.append(acc)
        if h == 1:
            o_t = jnp.concatenate([res[0][:MLA_V_DIM] / res[0][MLA_V_DIM:],
                                   res[1][MLA_V_DIM:] / res[1][:MLA_V_DIM]], axis=0)
            o_ref[c * MLA_Q_ROWS:(c + 1) * MLA_Q_ROWS, :] = o_t.T.astype(o_ref.dtype)
            res = []


def _mla(q, kc, vt):
    b, s, _ = q.shape
    return pl.pallas_call(
        _mla_kernel,
        grid=(b, N_PAIRS),
        in_specs=[pl.BlockSpec((None, s, MLA_PAIR_W), lambda bi, p: (bi, 0, p)),
                  pl.BlockSpec((None, s, MLA_PAIR_W), lambda bi, p: (bi, 0, p)),
                  pl.BlockSpec((None, PAIR, s), lambda bi, p: (bi, p, 0))],
        out_specs=pl.BlockSpec((None, s, PAIR), lambda bi, p: (bi, 0, p)),
        out_shape=jax.ShapeDtypeStruct((b, s, D_MLA_OUT), BF16),
        scratch_shapes=[pltpu.VMEM((2, MLA_Q_ROWS, s), F32)],
        compiler_params=pltpu.CompilerParams(
            dimension_semantics=("parallel", "parallel"),
            vmem_limit_bytes=V7X_VMEM_LIMIT_BYTES),
        name="mla",
    )(q, kc, vt)


DIL_SPAN = Q_BLOCK + 2 * HALF_WINDOW
DIL_MID = DIL_PAIRS[1][1]
assert DIL_PAIRS[0][1] == 1 and DIL_PAIRS[2][1] == DIL_MID * DIL_MID


def _band_bias():
    row = np.arange(Q_BLOCK)[:, None]
    col = np.arange(DIL_SPAN)[None, :]
    kinds = [np.abs(col - row - off) <= HALF_WINDOW
             for off in (0, HALF_WINDOW, 2 * HALF_WINDOW)]
    kinds.append((np.abs(col - row) <= HALF_WINDOW) & (col < Q_BLOCK))
    bias = np.where(np.stack(kinds), 0.0, MASK_VALUE).astype(np.float32)
    return np.concatenate([bias, bias], axis=1)


def _band_block(q_stack, k_span, v_ext, bias):
    s = lax.dot_general(q_stack, k_span, (((1,), (1,)), ((), ())),
                        preferred_element_type=F32) + bias
    m = jnp.max(s, axis=-1, keepdims=True)
    p = jnp.exp2((s - m).astype(BF16))
    res = jnp.dot(p, v_ext, preferred_element_type=F32)
    first = lax.broadcasted_iota(jnp.int32, (Q_BLOCK, PAIR), 1) < HEAD_DIM
    acc = jnp.where(first, res[:Q_BLOCK, :PAIR], res[Q_BLOCK:, :PAIR])
    l = jnp.where(first, res[:Q_BLOCK, PAIR:], res[Q_BLOCK:, PAIR:])
    m_lanes = jnp.where(first, jnp.broadcast_to(m[:Q_BLOCK], (Q_BLOCK, PAIR)),
                        jnp.broadcast_to(m[Q_BLOCK:], (Q_BLOCK, PAIR)))
    return acc, l, m_lanes


def _dil_kernel(q_ref, k_ref, v_ref, bias_ref, o_ref, mid_in_ref, nat_ref, mid_ref):
    s_len = q_ref.shape[0]
    mid_len = s_len // DIL_MID
    far_len = mid_len // DIL_MID
    first = lax.broadcasted_iota(jnp.int32, (1, PAIR), 1) < HEAD_DIM

    def run(q, k, v, sub_len, write):
        qa = jnp.where(first, q, 0.0).astype(BF16)
        qb = jnp.where(first, 0.0, q).astype(BF16)
        kb = k.astype(BF16)
        vb = v.astype(BF16)
        v_ext = jnp.concatenate([vb, jnp.ones_like(vb)], axis=1)
        span = min(DIL_SPAN, sub_len)
        for j in range(sub_len // Q_BLOCK):
            q0 = j * Q_BLOCK
            k0 = min(max(q0 - HALF_WINDOW, 0), sub_len - span)
            kind = 3 if span < DIL_SPAN else (q0 - k0) // HALF_WINDOW
            q_stack = jnp.concatenate([qa[q0:q0 + Q_BLOCK], qb[q0:q0 + Q_BLOCK]], axis=0)
            write(j, _band_block(q_stack, kb[k0:k0 + span], v_ext[k0:k0 + span],
                                 bias_ref[kind][:, :span]))

    def write_near(j, vals):
        for t, val in enumerate(vals):
            nat_ref[t, j * Q_BLOCK:(j + 1) * Q_BLOCK, :] = val
    run(q_ref[...], k_ref[...], v_ref[...], s_len, write_near)

    for t, ref in enumerate((q_ref, k_ref, v_ref)):
        for r in range(DIL_MID):
            mid_in_ref[t, r] = ref[pl.ds(r, mid_len, stride=DIL_MID), :]

    for r in range(DIL_MID):
        def write_mid(j, vals, r=r):
            for t, val in enumerate(vals):
                mid_ref[0, t, r, j * Q_BLOCK:(j + 1) * Q_BLOCK, :] = val
        run(mid_in_ref[0, r], mid_in_ref[1, r], mid_in_ref[2, r], mid_len, write_mid)

    for r in range(DIL_MID):
        for rr in range(DIL_MID):
            sl = pl.ds(rr, far_len, stride=DIL_MID)
            def write_far(j, vals, r=r, sl=sl):
                for t, val in enumerate(vals):
                    mid_ref[1, t, r, sl, :] = val
            run(mid_in_ref[0, r, sl, :], mid_in_ref[1, r, sl, :], mid_in_ref[2, r, sl, :],
                far_len, write_far)

    for r in range(DIL_MID):
        nat = pl.ds(r, mid_len, stride=DIL_MID)
        accs = (nat_ref[0, nat, :], mid_ref[0, 0, r], mid_ref[1, 0, r])
        ls = (nat_ref[1, nat, :], mid_ref[0, 1, r], mid_ref[1, 1, r])
        ms = (nat_ref[2, nat, :], mid_ref[0, 2, r], mid_ref[1, 2, r])
        top = jnp.maximum(jnp.maximum(ms[0], ms[1]), ms[2])
        es = [jnp.exp2(m - top) for m in ms]
        num = es[0] * accs[0] + es[1] * accs[1] + es[2] * accs[2]
        den = es[0] * ls[0] + es[1] * ls[1] + es[2] * ls[2]
        o_ref[nat, :] = num / den


def _dilated(qd, kd, vd):
    b, s, _ = qd.shape
    blk = pl.BlockSpec((None, s, PAIR), lambda bi, p: (bi, 0, p))
    bias = jnp.asarray(_band_bias())
    return pl.pallas_call(
        _dil_kernel,
        grid=(b, N_DIL_HEADS // 2),
        in_specs=[blk, blk, blk, _resident(bias.shape)],
        out_specs=blk,
        out_shape=jax.ShapeDtypeStruct((b, s, D_DIL_OUT), F32),
        scratch_shapes=[pltpu.VMEM((3, DIL_MID, s // DIL_MID, PAIR), F32),
                        pltpu.VMEM((3, s, PAIR), F32),
                        pltpu.VMEM((2, 3, DIL_MID, s // DIL_MID, PAIR), F32)],
        compiler_params=pltpu.CompilerParams(
            dimension_semantics=("parallel", "parallel"),
            vmem_limit_bytes=V7X_VMEM_LIMIT_BYTES),
        name="dilated",
    )(qd, kd, vd, bias)


def _rope_tables(seq, rot_dim):
    half = rot_dim // 2
    inv_freq = ROPE_THETA ** (-jnp.arange(half, dtype=F32) * (2.0 / rot_dim))
    ang = jnp.arange(seq, dtype=F32)[:, None] * inv_freq[None, :]
    return jnp.cos(ang), jnp.sin(ang)


def _rope_lane_tables(seq):
    cos_m, sin_m = _rope_tables(seq, MLA_ROPE_DIM)
    cos_p, sin_p = _rope_tables(seq, PARTIAL_ROT_DIM)
    zm, zp = jnp.zeros_like(sin_m), jnp.zeros_like(sin_p)

    def mla(first, second, fill):
        unit = jnp.concatenate([first, second], axis=-1)
        pad = jnp.full((seq, LANES - 2 * MLA_ROPE_DIM), fill, F32)
        return jnp.concatenate([unit, unit, pad], axis=-1)

    def dil(first, second, fill):
        rest = jnp.full((seq, HEAD_DIM - PARTIAL_ROT_DIM), fill, F32)
        head = jnp.concatenate([first, second, rest], axis=-1)
        return jnp.concatenate([head, head], axis=-1)

    return jnp.concatenate([
        mla(cos_m, cos_m, 1.0), mla(-sin_m, zm, 0.0), mla(zm, sin_m, 0.0),
        dil(cos_p, cos_p, 1.0), dil(-sin_p, zp, 0.0), dil(zp, sin_p, 0.0)], axis=-1)


def _reorder_w_in(w_in):
    c_q = w_in[:, :MLA_Q_LORA]
    c_kv = w_in[:, MLA_Q_LORA:MLA_Q_LORA + MLA_KV_LORA]
    k_r = w_in[:, MLA_Q_LORA + MLA_KV_LORA:MLA_Q_LORA + MLA_KV_LORA + MLA_ROPE_DIM]
    rest = w_in[:, MLA_Q_LORA + MLA_KV_LORA + MLA_ROPE_DIM:]
    pad = jnp.zeros((w_in.shape[0], LANES - 2 * MLA_ROPE_DIM), w_in.dtype)
    return jnp.concatenate([c_q, c_kv, k_r, k_r, pad, rest], axis=-1)


def _reorder_w_uq(w_uq):
    w = w_uq.reshape(MLA_Q_LORA, N_PAIRS, 2, MLA_NOPE_DIM + MLA_ROPE_DIM)
    nope = w[..., :MLA_NOPE_DIM].reshape(MLA_Q_LORA, N_PAIRS, 2 * MLA_NOPE_DIM)
    rope = w[..., MLA_NOPE_DIM:].reshape(MLA_Q_LORA, N_PAIRS, 2 * MLA_ROPE_DIM)
    pad = jnp.zeros((MLA_Q_LORA, N_PAIRS, LANES - 2 * MLA_ROPE_DIM), w_uq.dtype)
    return jnp.concatenate([nope, rope, pad], axis=-1).reshape(MLA_Q_LORA, N_PAIRS * MLA_PAIR_W)


def _split_w_ukv(w_ukv):
    w = w_ukv.reshape(MLA_KV_LORA, N_MLA_HEADS, MLA_NOPE_DIM + MLA_V_DIM)
    k_nope = w[..., :MLA_NOPE_DIM].reshape(MLA_KV_LORA, D_MLA_OUT)
    v = w[..., MLA_NOPE_DIM:].reshape(MLA_KV_LORA, D_MLA_OUT)
    return k_nope, v.T


def kernel(x, ffn1_norm, ffn1_w_gate, ffn1_w_up, ffn1_w_down, mix_norm, w_in, mla_q_norm, mla_w_uq, mla_kv_norm, mla_w_ukv, mla_out_norm, dil_out_norm, w_out, ffn2_norm, ffn2_w_gate, ffn2_w_up, ffn2_w_down, final_norm):
    b, s, d = x.shape
    assert ffn1_norm.shape[0] == 1, "single-layer block: the final norm is fused into its last FFN"
    assert d == D_MODEL and s % MLA_Q_ROWS == 0 and s % (Q_BLOCK * DIL_PAIRS[-1][1]) == 0
    tab = _rope_lane_tables(s)
    x1 = _ffn1(x.reshape(b * s, d), ffn1_norm[0][None], ffn1_w_gate[0].astype(BF16),
               ffn1_w_up[0].astype(BF16), ffn1_w_down[0].astype(BF16))
    w_uk, w_uvt = _split_w_ukv(mla_w_ukv[0])
    q, kc, vt, qd, kd, vd = _proj(
        x1.reshape(b, s, d), mix_norm[0][None], _reorder_w_in(w_in[0]).astype(BF16),
        mla_q_norm[0][None], _reorder_w_uq(mla_w_uq[0]).astype(BF16),
        mla_kv_norm[0][None], w_uk.astype(BF16), w_uvt.astype(BF16), tab)
    o_mla = _mla(q, kc, vt).reshape(b * s, D_MLA_OUT)
    o_dil = _dilated(qd, kd, vd).reshape(b * s, D_DIL_OUT)
    wo = w_out[0].astype(BF16)
    y = _ffn2(x1, o_mla, o_dil, mla_out_norm[0][None], dil_out_norm[0][None],
              wo[:D_MLA_OUT], wo[D_MLA_OUT:], ffn2_norm[0][None],
              ffn2_w_gate[0].astype(BF16), ffn2_w_up[0].astype(BF16),
              ffn2_w_down[0].astype(BF16), final_norm[None])
    return y.reshape(b, s, d)
```

```python
import functools

import numpy as np
import jax
import jax.numpy as jnp
from jax import lax
from jax.experimental import pallas as pl
from jax.experimental.pallas import tpu as pltpu

F32 = jnp.float32
BF16 = jnp.bfloat16

D_MODEL = 1024
HEAD_DIM = 64
N_MLA_HEADS = 8
N_DIL_HEADS = 8
MLA_NOPE_DIM = 64
MLA_ROPE_DIM = 32
MLA_V_DIM = 64
MLA_Q_LORA = 384
MLA_KV_LORA = 256
D_MLA_OUT = N_MLA_HEADS * MLA_V_DIM
D_DIL_OUT = N_DIL_HEADS * HEAD_DIM
DIL_PAIRS = ((128, 1), (512, 4), (2048, 16))
PARTIAL_ROT_DIM = 16
ROPE_THETA = 500000.0
D_FF = 2816
Q_BLOCK = 128
NORM_EPS = 1e-6
MASK_VALUE = -1e30
HALF_WINDOW = 64
assert all(w // (2 * d) == HALF_WINDOW for w, d in DIL_PAIRS)
LOG2_E = 1.4426950408889634

LANES = 128
V7X_VMEM_LIMIT_BYTES = 56 * 1024 * 1024

FFN_ROWS = 512
PROJ_ROWS = 512
MLA_Q_ROWS = 512
MLA_KEY_TILE = 256
PAIR = 2 * HEAD_DIM
N_PAIRS = N_MLA_HEADS // 2
MLA_PAIR_W = 2 * LANES

_OFF_CQ = 0
_OFF_CKV = _OFF_CQ + MLA_Q_LORA
_OFF_KR = _OFF_CKV + MLA_KV_LORA
_OFF_QD = _OFF_KR + LANES
_OFF_KD = _OFF_QD + D_DIL_OUT
_OFF_VD = _OFF_KD + D_DIL_OUT
_PROJ_W = _OFF_VD + D_DIL_OUT


def _rms(x, g):
    ms = jnp.mean(x * x, axis=-1, keepdims=True)
    return x * lax.rsqrt(ms + NORM_EPS) * g


def _rope(x, c, sa, sb, half):
    return x * c + pltpu.roll(x, LANES - half, 1) * sa + pltpu.roll(x, half, 1) * sb


def _swiglu_residual(x, g_ref, wg_ref, wu_ref, wd_ref):
    h = _rms(x, g_ref[...]).astype(BF16)
    gate = jnp.dot(h, wg_ref[...], preferred_element_type=F32)
    up = jnp.dot(h, wu_ref[...], preferred_element_type=F32)
    act = (gate * jax.nn.sigmoid(gate) * up).astype(BF16)
    return x + 0.5 * jnp.dot(act, wd_ref[...], preferred_element_type=F32)


def _ffn1_kernel(x_ref, g_ref, wg_ref, wu_ref, wd_ref, o_ref):
    o_ref[...] = _swiglu_residual(x_ref[...], g_ref, wg_ref, wu_ref, wd_ref)


def _ffn2_kernel(x_ref, om_ref, od_ref, gm_ref, gd_ref, woa_ref, wob_ref,
                 g_ref, wg_ref, wu_ref, wd_ref, gf_ref, o_ref):
    nm = _rms(om_ref[...].astype(F32), gm_ref[...]).astype(BF16)
    nd = _rms(od_ref[...].astype(F32), gd_ref[...]).astype(BF16)
    x = (x_ref[...]
         + jnp.dot(nm, woa_ref[...], preferred_element_type=F32)
         + jnp.dot(nd, wob_ref[...], preferred_element_type=F32))
    o_ref[...] = _rms(_swiglu_residual(x, g_ref, wg_ref, wu_ref, wd_ref), gf_ref[...])


def _resident(shape):
    return pl.BlockSpec(shape, lambda *_: (0,) * len(shape), pipeline_mode=pl.Buffered(1))


def _ffn1(x2d, g, wg, wu, wd):
    n = x2d.shape[0]
    row = pl.BlockSpec((FFN_ROWS, D_MODEL), lambda i: (i, 0))
    return pl.pallas_call(
        _ffn1_kernel,
        grid=(n // FFN_ROWS,),
        in_specs=[row, _resident((1, D_MODEL)), _resident((D_MODEL, D_FF)),
                  _resident((D_MODEL, D_FF)), _resident((D_FF, D_MODEL))],
        out_specs=row,
        out_shape=jax.ShapeDtypeStruct((n, D_MODEL), F32),
        compiler_params=pltpu.CompilerParams(
            dimension_semantics=("parallel",), vmem_limit_bytes=V7X_VMEM_LIMIT_BYTES),
        name="ffn1",
    )(x2d, g, wg, wu, wd)


def _ffn2(x2d, om, od, gm, gd, woa, wob, g, wg, wu, wd, gf):
    n = x2d.shape[0]
    row = pl.BlockSpec((FFN_ROWS, D_MODEL), lambda i: (i, 0))
    half_row = pl.BlockSpec((FFN_ROWS, D_MLA_OUT), lambda i: (i, 0))
    return pl.pallas_call(
        _ffn2_kernel,
        grid=(n // FFN_ROWS,),
        in_specs=[row, half_row, half_row,
                  _resident((1, D_MLA_OUT)), _resident((1, D_DIL_OUT)),
                  _resident((D_MLA_OUT, D_MODEL)), _resident((D_DIL_OUT, D_MODEL)),
                  _resident((1, D_MODEL)), _resident((D_MODEL, D_FF)),
                  _resident((D_MODEL, D_FF)), _resident((D_FF, D_MODEL)),
                  _resident((1, D_MODEL))],
        out_specs=row,
        out_shape=jax.ShapeDtypeStruct((n, D_MODEL), F32),
        compiler_params=pltpu.CompilerParams(
            dimension_semantics=("parallel",), vmem_limit_bytes=V7X_VMEM_LIMIT_BYTES),
        name="ffn2",
    )(x2d, om, od, gm, gd, woa, wob, g, wg, wu, wd, gf)


def _proj_kernel(x_ref, g_ref, win_ref, gq_ref, wuq_ref, gkv_ref, wukv_ref, tab_ref,
                 q_ref, kc_ref, v_ref, qd_ref, kd_ref, vd_ref):
    h = _rms(x_ref[...], g_ref[...]).astype(BF16)
    proj = jnp.dot(h, win_ref[...], preferred_element_type=F32)

    cm, sam, sbm = (tab_ref[:, i * LANES:(i + 1) * LANES] for i in range(3))
    cd, sad, sbd = (tab_ref[:, i * LANES:(i + 1) * LANES] for i in range(3, 6))
    rope_m = functools.partial(_rope, c=cm, sa=sam, sb=sbm, half=MLA_ROPE_DIM // 2)
    rope_d = functools.partial(_rope, c=cd, sa=sad, sb=sbd, half=PARTIAL_ROT_DIM // 2)

    cq = _rms(proj[:, _OFF_CQ:_OFF_CQ + MLA_Q_LORA], gq_ref[...]).astype(BF16)
    q = jnp.dot(cq, wuq_ref[...], preferred_element_type=F32)
    q_scale = LOG2_E * (MLA_NOPE_DIM + MLA_ROPE_DIM) ** -0.5
    for p in range(N_PAIRS):
        lo = p * MLA_PAIR_W
        q_ref[:, lo:lo + LANES] = (q[:, lo:lo + LANES] * q_scale).astype(BF16)
        q_ref[:, lo + LANES:lo + 2 * LANES] = (
            rope_m(q[:, lo + LANES:lo + 2 * LANES]) * q_scale).astype(BF16)

    ckv = _rms(proj[:, _OFF_CKV:_OFF_CKV + MLA_KV_LORA], gkv_ref[...]).astype(BF16)
    kv = jnp.dot(ckv, wukv_ref[...], preferred_element_type=F32)
    k_rope = rope_m(proj[:, _OFF_KR:_OFF_KR + LANES]).astype(BF16)
    for p in range(N_PAIRS):
        lo = p * MLA_PAIR_W
        kc_ref[:, lo:lo + LANES] = kv[:, p * PAIR:(p + 1) * PAIR].astype(BF16)
        kc_ref[:, lo + LANES:lo + 2 * LANES] = k_rope
    v_ref[...] = kv[:, D_MLA_OUT:].astype(BF16)

    d_scale = LOG2_E * HEAD_DIM ** -0.5
    for p in range(N_DIL_HEADS // 2):
        lo = p * PAIR
        qd_ref[:, lo:lo + PAIR] = rope_d(proj[:, _OFF_QD + lo:_OFF_QD + lo + PAIR]) * d_scale
        kd_ref[:, lo:lo + PAIR] = rope_d(proj[:, _OFF_KD + lo:_OFF_KD + lo + PAIR])
    vd_ref[...] = proj[:, _OFF_VD:_OFF_VD + D_DIL_OUT]


def _proj(x3d, g, win, gq, wuq, gkv, wukv, tab):
    b, s, _ = x3d.shape

    def tok(width):
        return pl.BlockSpec((None, PROJ_ROWS, width), lambda si, bi: (bi, si, 0))

    def out(width, dtype):
        return jax.ShapeDtypeStruct((b, s, width), dtype)

    return pl.pallas_call(
        _proj_kernel,
        grid=(s // PROJ_ROWS, b),
        in_specs=[tok(D_MODEL), _resident((1, D_MODEL)), _resident((D_MODEL, _PROJ_W)),
                  _resident((1, MLA_Q_LORA)), _resident((MLA_Q_LORA, N_PAIRS * MLA_PAIR_W)),
                  _resident((1, MLA_KV_LORA)), _resident((MLA_KV_LORA, 2 * D_MLA_OUT)),
                  pl.BlockSpec((PROJ_ROWS, 6 * LANES), lambda si, bi: (si, 0))],
        out_specs=[tok(N_PAIRS * MLA_PAIR_W), tok(N_PAIRS * MLA_PAIR_W), tok(D_MLA_OUT),
                   tok(D_DIL_OUT), tok(D_DIL_OUT), tok(D_DIL_OUT)],
        out_shape=[out(N_PAIRS * MLA_PAIR_W, BF16), out(N_PAIRS * MLA_PAIR_W, BF16),
                   out(D_MLA_OUT, BF16), out(D_DIL_OUT, F32), out(D_DIL_OUT, F32),
                   out(D_DIL_OUT, F32)],
        compiler_params=pltpu.CompilerParams(
            dimension_semantics=("parallel", "parallel"),
            vmem_limit_bytes=V7X_VMEM_LIMIT_BYTES),
        name="proj",
    )(x3d, g, win, gq, wuq, gkv, wukv, tab)


def _mla_kernel(q_ref, kc_ref, v_ref, o_ref, s_ref):
    s_len = kc_ref.shape[0]
    n_tiles = s_len // MLA_KEY_TILE
    v = v_ref[...]
    first_v = lax.broadcasted_iota(jnp.int32, v.shape, 1) < MLA_V_DIM
    one = jnp.ones_like(v)
    v_heads = (jnp.where(first_v, v, one), jnp.where(first_v, one, v))
    lane_q = lax.broadcasted_iota(jnp.int32, (MLA_Q_ROWS, MLA_PAIR_W), 1)
    first_o = lax.broadcasted_iota(jnp.int32, (MLA_Q_ROWS, PAIR), 1) < MLA_V_DIM
    chains = [(c, h) for c in range(s_len // MLA_Q_ROWS) for h in range(2)]

    def masked_q(c, h):
        q = q_ref[c * MLA_Q_ROWS:(c + 1) * MLA_Q_ROWS, :]
        nope = (lane_q >= h * MLA_NOPE_DIM) & (lane_q < (h + 1) * MLA_NOPE_DIM)
        rope = ((lane_q >= LANES + h * MLA_ROPE_DIM)
                & (lane_q < LANES + (h + 1) * MLA_ROPE_DIM))
        return jnp.where(nope | rope, q, jnp.zeros_like(q))

    def score_tile(slot, qh, t, run_max):
        keys = slice(t * MLA_KEY_TILE, (t + 1) * MLA_KEY_TILE)
        s_t = lax.dot_general(qh, kc_ref[keys, :], (((1,), (1,)), ((), ())),
                              preferred_element_type=F32)
        s_ref[slot, :, keys] = s_t
        tile_max = functools.reduce(
            jnp.maximum, [s_t[:, i:i + LANES] for i in range(0, MLA_KEY_TILE, LANES)])
        return tile_max if run_max is None else jnp.maximum(run_max, tile_max)

    def pv_tile(slot, h, t, m, acc):
        keys = slice(t * MLA_KEY_TILE, (t + 1) * MLA_KEY_TILE)
        p_t = jnp.exp2((s_ref[slot, :, keys] - m).astype(BF16))
        d = jnp.dot(p_t, v_heads[h][keys, :], preferred_element_type=F32)
        return d if acc is None else acc + d

    qh = masked_q(*chains[0])
    run_max = None
    for t in range(n_tiles):
        run_max = score_tile(0, qh, t, run_max)
    m = jnp.max(run_max, axis=-1, keepdims=True)
    res = []
    for i, (c, h) in enumerate(chains):
        nxt = chains[i + 1] if i + 1 < len(chains) else None
        if nxt is not None:
            qh = masked_q(*nxt)
        run_max, acc = None, None
        for t in range(n_tiles):
            if nxt is not None:
                run_max = score_tile((i + 1) % 2, qh, t, run_max)
            acc = pv_tile(i % 2, h, t, m, acc)
        if nxt is not None:
            m = jnp.max(run_max, axis=-1, keepdims=True)
        res.append(acc)
        if h == 1:
            num = jnp.where(first_o, res[0], res[1])
            den = pltpu.roll(jnp.where(first_o, res[1], res[0]), MLA_V_DIM, 1)
            o_ref[c * MLA_Q_ROWS:(c + 1) * MLA_Q_ROWS, :] = (num / den).astype(o_ref.dtype)
            res = []


def _mla(q, kc, v):
    b, s, _ = q.shape
    return pl.pallas_call(
        _mla_kernel,
        grid=(b, N_PAIRS),
        in_specs=[pl.BlockSpec((None, s, MLA_PAIR_W), lambda bi, p: (bi, 0, p)),
                  pl.BlockSpec((None, s, MLA_PAIR_W), lambda bi, p: (bi, 0, p)),
                  pl.BlockSpec((None, s, PAIR), lambda bi, p: (bi, 0, p))],
        out_specs=pl.BlockSpec((None, s, PAIR), lambda bi, p: (bi, 0, p)),
        out_shape=jax.ShapeDtypeStruct((b, s, D_MLA_OUT), BF16),
        scratch_shapes=[pltpu.VMEM((2, MLA_Q_ROWS, s), F32)],
        compiler_params=pltpu.CompilerParams(
            dimension_semantics=("parallel", "parallel"),
            vmem_limit_bytes=V7X_VMEM_LIMIT_BYTES),
        name="mla",
    )(q, kc, v)


DIL_SPAN = Q_BLOCK + 2 * HALF_WINDOW
DIL_MID = DIL_PAIRS[1][1]
assert DIL_PAIRS[0][1] == 1 and DIL_PAIRS[2][1] == DIL_MID * DIL_MID


def _band_bias():
    row = np.arange(Q_BLOCK)[:, None]
    col = np.arange(DIL_SPAN)[None, :]
    kinds = [np.abs(col - row - off) <= HALF_WINDOW
             for off in (0, HALF_WINDOW, 2 * HALF_WINDOW)]
    kinds.append((np.abs(col - row) <= HALF_WINDOW) & (col < Q_BLOCK))
    bias = np.where(np.stack(kinds), 0.0, MASK_VALUE).astype(np.float32)
    return np.concatenate([bias, bias], axis=1)


def _band_scores(q_stack, k_span, bias):
    return lax.dot_general(q_stack, k_span, (((1,), (1,)), ((), ())),
                           preferred_element_type=F32) + bias


def _band_finish(s, v_ext):
    m = jnp.max(s, axis=-1, keepdims=True)
    p = jnp.exp2((s - m).astype(BF16))
    res = jnp.dot(p, v_ext, preferred_element_type=F32)
    first = lax.broadcasted_iota(jnp.int32, (Q_BLOCK, PAIR), 1) < HEAD_DIM
    acc = jnp.where(first, res[:Q_BLOCK, :PAIR], res[Q_BLOCK:, :PAIR])
    l = jnp.where(first, res[:Q_BLOCK, PAIR:], res[Q_BLOCK:, PAIR:])
    m_lanes = jnp.where(first, jnp.broadcast_to(m[:Q_BLOCK], (Q_BLOCK, PAIR)),
                        jnp.broadcast_to(m[Q_BLOCK:], (Q_BLOCK, PAIR)))
    return acc, l, m_lanes


def _dil_kernel(q_ref, k_ref, v_ref, bias_ref, o_ref, mid_in_ref, nat_ref, mid_ref):
    s_len = q_ref.shape[0]
    mid_len = s_len // DIL_MID
    far_len = mid_len // DIL_MID
    first = lax.broadcasted_iota(jnp.int32, (1, PAIR), 1) < HEAD_DIM

    for t, ref in enumerate((q_ref, k_ref, v_ref)):
        for r in range(DIL_MID):
            mid_in_ref[t, r] = ref[pl.ds(r, mid_len, stride=DIL_MID), :]

    groups = []

    def write_near(j, vals):
        for t, val in enumerate(vals):
            nat_ref[t, j * Q_BLOCK:(j + 1) * Q_BLOCK, :] = val
    groups.append((lambda: (q_ref[...], k_ref[...], v_ref[...]), s_len, write_near))

    for r in range(DIL_MID):
        def write_mid(j, vals, r=r):
            for t, val in enumerate(vals):
                mid_ref[0, t, r, j * Q_BLOCK:(j + 1) * Q_BLOCK, :] = val
        groups.append((lambda r=r: tuple(mid_in_ref[t, r] for t in range(3)), mid_len, write_mid))

    for r in range(DIL_MID):
        for rr in range(DIL_MID):
            sl = pl.ds(rr, far_len, stride=DIL_MID)
            def write_far(j, vals, r=r, sl=sl):
                for t, val in enumerate(vals):
                    mid_ref[1, t, r, sl, :] = val
            groups.append((lambda r=r, sl=sl: tuple(mid_in_ref[t, r, sl, :] for t in range(3)),
                           far_len, write_far))

    blocks = [(g, j) for g, (_, sub_len, _) in enumerate(groups)
              for j in range(sub_len // Q_BLOCK)]
    operands = {}

    def scores(g, j):
        load, sub_len, _ = groups[g]
        if g not in operands:
            q, k, v = load()
            vb = v.astype(BF16)
            operands.clear()
            operands[g] = (jnp.where(first, q, 0.0).astype(BF16),
                           jnp.where(first, 0.0, q).astype(BF16),
                           k.astype(BF16), jnp.concatenate([vb, jnp.ones_like(vb)], axis=1))
        qa, qb, kb, v_ext = operands[g]
        span = min(DIL_SPAN, sub_len)
        q0 = j * Q_BLOCK
        k0 = min(max(q0 - HALF_WINDOW, 0), sub_len - span)
        kind = 3 if span < DIL_SPAN else (q0 - k0) // HALF_WINDOW
        q_stack = jnp.concatenate([qa[q0:q0 + Q_BLOCK], qb[q0:q0 + Q_BLOCK]], axis=0)
        return (_band_scores(q_stack, kb[k0:k0 + span], bias_ref[kind][:, :span]),
                v_ext[k0:k0 + span])

    ahead = scores(*blocks[0])
    for i, (g, j) in enumerate(blocks):
        s, v_span = ahead
        if i + 1 < len(blocks):
            ahead = scores(*blocks[i + 1])
        groups[g][2](j, _band_finish(s, v_span))

    for r in range(DIL_MID):
        nat = pl.ds(r, mid_len, stride=DIL_MID)
        accs = (nat_ref[0, nat, :], mid_ref[0, 0, r], mid_ref[1, 0, r])
        ls = (nat_ref[1, nat, :], mid_ref[0, 1, r], mid_ref[1, 1, r])
        ms = (nat_ref[2, nat, :], mid_ref[0, 2, r], mid_ref[1, 2, r])
        top = jnp.maximum(jnp.maximum(ms[0], ms[1]), ms[2])
        es = [jnp.exp2(m - top) for m in ms]
        num = es[0] * accs[0] + es[1] * accs[1] + es[2] * accs[2]
        den = es[0] * ls[0] + es[1] * ls[1] + es[2] * ls[2]
        o_ref[nat, :] = num / den


def _dilated(qd, kd, vd):
    b, s, _ = qd.shape
    blk = pl.BlockSpec((None, s, PAIR), lambda bi, p: (bi, 0, p))
    bias = jnp.asarray(_band_bias())
    return pl.pallas_call(
        _dil_kernel,
        grid=(b, N_DIL_HEADS // 2),
        in_specs=[blk, blk, blk, _resident(bias.shape)],
        out_specs=blk,
        out_shape=jax.ShapeDtypeStruct((b, s, D_DIL_OUT), F32),
        scratch_shapes=[pltpu.VMEM((3, DIL_MID, s // DIL_MID, PAIR), F32),
                        pltpu.VMEM((3, s, PAIR), F32),
                        pltpu.VMEM((2, 3, DIL_MID, s // DIL_MID, PAIR), F32)],
        compiler_params=pltpu.CompilerParams(
            dimension_semantics=("parallel", "parallel"),
            vmem_limit_bytes=V7X_VMEM_LIMIT_BYTES),
        name="dilated",
    )(qd, kd, vd, bias)


def _rope_tables(seq, rot_dim):
    half = rot_dim // 2
    inv_freq = ROPE_THETA ** (-jnp.arange(half, dtype=F32) * (2.0 / rot_dim))
    ang = jnp.arange(seq, dtype=F32)[:, None] * inv_freq[None, :]
    return jnp.cos(ang), jnp.sin(ang)


def _rope_lane_tables(seq):
    cos_m, sin_m = _rope_tables(seq, MLA_ROPE_DIM)
    cos_p, sin_p = _rope_tables(seq, PARTIAL_ROT_DIM)
    zm, zp = jnp.zeros_like(sin_m), jnp.zeros_like(sin_p)

    def mla(first, second, fill):
        unit = jnp.concatenate([first, second], axis=-1)
        pad = jnp.full((seq, LANES - 2 * MLA_ROPE_DIM), fill, F32)
        return jnp.concatenate([unit, unit, pad], axis=-1)

    def dil(first, second, fill):
        rest = jnp.full((seq, HEAD_DIM - PARTIAL_ROT_DIM), fill, F32)
        head = jnp.concatenate([first, second, rest], axis=-1)
        return jnp.concatenate([head, head], axis=-1)

    return jnp.concatenate([
        mla(cos_m, cos_m, 1.0), mla(-sin_m, zm, 0.0), mla(zm, sin_m, 0.0),
        dil(cos_p, cos_p, 1.0), dil(-sin_p, zp, 0.0), dil(zp, sin_p, 0.0)], axis=-1)


def _reorder_w_in(w_in):
    c_q = w_in[:, :MLA_Q_LORA]
    c_kv = w_in[:, MLA_Q_LORA:MLA_Q_LORA + MLA_KV_LORA]
    k_r = w_in[:, MLA_Q_LORA + MLA_KV_LORA:MLA_Q_LORA + MLA_KV_LORA + MLA_ROPE_DIM]
    rest = w_in[:, MLA_Q_LORA + MLA_KV_LORA + MLA_ROPE_DIM:]
    pad = jnp.zeros((w_in.shape[0], LANES - 2 * MLA_ROPE_DIM), w_in.dtype)
    return jnp.concatenate([c_q, c_kv, k_r, k_r, pad, rest], axis=-1)


def _reorder_w_uq(w_uq):
    w = w_uq.reshape(MLA_Q_LORA, N_PAIRS, 2, MLA_NOPE_DIM + MLA_ROPE_DIM)
    nope = w[..., :MLA_NOPE_DIM].reshape(MLA_Q_LORA, N_PAIRS, 2 * MLA_NOPE_DIM)
    rope = w[..., MLA_NOPE_DIM:].reshape(MLA_Q_LORA, N_PAIRS, 2 * MLA_ROPE_DIM)
    pad = jnp.zeros((MLA_Q_LORA, N_PAIRS, LANES - 2 * MLA_ROPE_DIM), w_uq.dtype)
    return jnp.concatenate([nope, rope, pad], axis=-1).reshape(MLA_Q_LORA, N_PAIRS * MLA_PAIR_W)


def _reorder_w_ukv(w_ukv):
    w = w_ukv.reshape(MLA_KV_LORA, N_MLA_HEADS, MLA_NOPE_DIM + MLA_V_DIM)
    k_nope = w[..., :MLA_NOPE_DIM].reshape(MLA_KV_LORA, D_MLA_OUT)
    v = w[..., MLA_NOPE_DIM:].reshape(MLA_KV_LORA, D_MLA_OUT)
    return jnp.concatenate([k_nope, v], axis=-1)


def kernel(x, ffn1_norm, ffn1_w_gate, ffn1_w_up, ffn1_w_down, mix_norm, w_in, mla_q_norm, mla_w_uq, mla_kv_norm, mla_w_ukv, mla_out_norm, dil_out_norm, w_out, ffn2_norm, ffn2_w_gate, ffn2_w_up, ffn2_w_down, final_norm):
    b, s, d = x.shape
    assert ffn1_norm.shape[0] == 1, "single-layer block: the final norm is fused into its last FFN"
    assert d == D_MODEL and s % MLA_Q_ROWS == 0 and s % (Q_BLOCK * DIL_PAIRS[-1][1]) == 0
    tab = _rope_lane_tables(s)
    x1 = _ffn1(x.reshape(b * s, d), ffn1_norm[0][None], ffn1_w_gate[0].astype(BF16),
               ffn1_w_up[0].astype(BF16), ffn1_w_down[0].astype(BF16))
    q, kc, v, qd, kd, vd = _proj(
        x1.reshape(b, s, d), mix_norm[0][None], _reorder_w_in(w_in[0]).astype(BF16),
        mla_q_norm[0][None], _reorder_w_uq(mla_w_uq[0]).astype(BF16),
        mla_kv_norm[0][None], _reorder_w_ukv(mla_w_ukv[0]).astype(BF16), tab)
    o_mla = _mla(q, kc, v).reshape(b * s, D_MLA_OUT)
    o_dil = _dilated(qd, kd, vd).reshape(b * s, D_DIL_OUT)
    wo = w_out[0].astype(BF16)
    y = _ffn2(x1, o_mla, o_dil, mla_out_norm[0][None], dil_out_norm[0][None],
              wo[:D_MLA_OUT], wo[D_MLA_OUT:], ffn2_norm[0][None],
              ffn2_w_gate[0].astype(BF16), ffn2_w_up[0].astype(BF16),
              ffn2_w_down[0].astype(BF16), final_norm[None])
    return y.reshape(b, s, d)
```

```python
import functools

import numpy as np
import jax
import jax.numpy as jnp
from jax import lax
from jax.experimental import pallas as pl
from jax.experimental.pallas import tpu as pltpu

F32 = jnp.float32
BF16 = jnp.bfloat16

D_MODEL = 1024
HEAD_DIM = 64
N_MLA_HEADS = 8
N_DIL_HEADS = 8
MLA_NOPE_DIM = 64
MLA_ROPE_DIM = 32
MLA_V_DIM = 64
MLA_Q_LORA = 384
MLA_KV_LORA = 256
D_MLA_OUT = N_MLA_HEADS * MLA_V_DIM
D_DIL_OUT = N_DIL_HEADS * HEAD_DIM
DIL_PAIRS = ((128, 1), (512, 4), (2048, 16))
PARTIAL_ROT_DIM = 16
ROPE_THETA = 500000.0
D_FF = 2816
Q_BLOCK = 128
NORM_EPS = 1e-6
MASK_VALUE = -1e30
HALF_WINDOW = 64
assert all(w // (2 * d) == HALF_WINDOW for w, d in DIL_PAIRS)
LOG2_E = 1.4426950408889634

LANES = 128
V7X_VMEM_LIMIT_BYTES = 56 * 1024 * 1024

FFN_ROWS = 1024
FFN_HIDDEN_CHUNK = 1024
PROJ_ROWS = 1024
MLA_Q_ROWS = 512
MLA_KEY_TILE = 256
PAIR = 2 * HEAD_DIM
N_PAIRS = N_MLA_HEADS // 2
MLA_PAIR_W = 2 * LANES

_OFF_CQ = 0
_OFF_CKV = _OFF_CQ + MLA_Q_LORA
_OFF_KR = _OFF_CKV + MLA_KV_LORA
_OFF_QD = _OFF_KR + LANES
_OFF_KD = _OFF_QD + D_DIL_OUT
_OFF_VD = _OFF_KD + D_DIL_OUT
_PROJ_W = _OFF_VD + D_DIL_OUT


def _rms(x, g):
    ms = jnp.mean(x * x, axis=-1, keepdims=True)
    return x * lax.rsqrt(ms + NORM_EPS) * g


def _rope(x, c, sa, sb, half):
    return x * c + pltpu.roll(x, LANES - half, 1) * sa + pltpu.roll(x, half, 1) * sb


def _swiglu_residual(x, g_ref, wg_ref, wu_ref, wd_ref):
    h = _rms(x, g_ref[...]).astype(BF16)
    out = None
    for lo in range(0, D_FF, FFN_HIDDEN_CHUNK):
        hi = min(lo + FFN_HIDDEN_CHUNK, D_FF)
        gate = jnp.dot(h, wg_ref[:, lo:hi], preferred_element_type=F32)
        up = jnp.dot(h, wu_ref[:, lo:hi], preferred_element_type=F32)
        act = (gate * jax.nn.sigmoid(gate) * up).astype(BF16)
        part = jnp.dot(act, wd_ref[lo:hi, :], preferred_element_type=F32)
        out = part if out is None else out + part
    return x + 0.5 * out


def _ffn1_kernel(x_ref, g_ref, wg_ref, wu_ref, wd_ref, o_ref):
    o_ref[...] = _swiglu_residual(x_ref[...], g_ref, wg_ref, wu_ref, wd_ref)


def _ffn2_kernel(x_ref, om_ref, od_ref, gm_ref, gd_ref, wo_ref,
                 g_ref, wg_ref, wu_ref, wd_ref, gf_ref, o_ref):
    nm = _rms(om_ref[...].astype(F32), gm_ref[...]).astype(BF16)
    nd = _rms(od_ref[...].astype(F32), gd_ref[...]).astype(BF16)
    x = (x_ref[...]
         + jnp.dot(nm, wo_ref[:D_MLA_OUT, :], preferred_element_type=F32)
         + jnp.dot(nd, wo_ref[D_MLA_OUT:, :], preferred_element_type=F32))
    o_ref[...] = _rms(_swiglu_residual(x, g_ref, wg_ref, wu_ref, wd_ref), gf_ref[...])


def _resident(shape):
    return pl.BlockSpec(shape, lambda *_: (0,) * len(shape), pipeline_mode=pl.Buffered(1))


def _ffn1(x2d, g, wg, wu, wd):
    n = x2d.shape[0]
    row = pl.BlockSpec((FFN_ROWS, D_MODEL), lambda i: (i, 0))
    return pl.pallas_call(
        _ffn1_kernel,
        grid=(n // FFN_ROWS,),
        in_specs=[row, _resident((1, D_MODEL)), _resident((D_MODEL, D_FF)),
                  _resident((D_MODEL, D_FF)), _resident((D_FF, D_MODEL))],
        out_specs=row,
        out_shape=jax.ShapeDtypeStruct((n, D_MODEL), F32),
        compiler_params=pltpu.CompilerParams(
            dimension_semantics=("parallel",), vmem_limit_bytes=V7X_VMEM_LIMIT_BYTES),
        name="ffn1",
    )(x2d, g, wg, wu, wd)


def _ffn2(x2d, om, od, gm, gd, wo, g, wg, wu, wd, gf):
    n = x2d.shape[0]
    row = pl.BlockSpec((FFN_ROWS, D_MODEL), lambda i: (i, 0))
    half_row = pl.BlockSpec((FFN_ROWS, D_MLA_OUT), lambda i: (i, 0))
    return pl.pallas_call(
        _ffn2_kernel,
        grid=(n // FFN_ROWS,),
        in_specs=[row, half_row, half_row,
                  _resident((1, D_MLA_OUT)), _resident((1, D_DIL_OUT)),
                  _resident((D_MLA_OUT + D_DIL_OUT, D_MODEL)),
                  _resident((1, D_MODEL)), _resident((D_MODEL, D_FF)),
                  _resident((D_MODEL, D_FF)), _resident((D_FF, D_MODEL)),
                  _resident((1, D_MODEL))],
        out_specs=row,
        out_shape=jax.ShapeDtypeStruct((n, D_MODEL), F32),
        compiler_params=pltpu.CompilerParams(
            dimension_semantics=("parallel",), vmem_limit_bytes=V7X_VMEM_LIMIT_BYTES),
        name="ffn2",
    )(x2d, om, od, gm, gd, wo, g, wg, wu, wd, gf)


def _proj_kernel(x_ref, g_ref, win_ref, gq_ref, wuq_ref, gkv_ref, wukv_ref, tab_ref,
                 q_ref, kc_ref, v_ref, qd_ref, kd_ref, vd_ref):
    h = _rms(x_ref[...], g_ref[...]).astype(BF16)
    proj = jnp.dot(h, win_ref[...], preferred_element_type=F32)

    cm, sam, sbm = (tab_ref[:, i * LANES:(i + 1) * LANES] for i in range(3))
    cd, sad, sbd = (tab_ref[:, i * LANES:(i + 1) * LANES] for i in range(3, 6))
    rope_m = functools.partial(_rope, c=cm, sa=sam, sb=sbm, half=MLA_ROPE_DIM // 2)
    rope_d = functools.partial(_rope, c=cd, sa=sad, sb=sbd, half=PARTIAL_ROT_DIM // 2)

    cq = _rms(proj[:, _OFF_CQ:_OFF_CQ + MLA_Q_LORA], gq_ref[...]).astype(BF16)
    q = jnp.dot(cq, wuq_ref[...], preferred_element_type=F32)
    q_scale = LOG2_E * (MLA_NOPE_DIM + MLA_ROPE_DIM) ** -0.5
    for p in range(N_PAIRS):
        lo = p * MLA_PAIR_W
        q_ref[:, lo:lo + LANES] = (q[:, lo:lo + LANES] * q_scale).astype(BF16)
        q_ref[:, lo + LANES:lo + 2 * LANES] = (
            rope_m(q[:, lo + LANES:lo + 2 * LANES]) * q_scale).astype(BF16)

    ckv = _rms(proj[:, _OFF_CKV:_OFF_CKV + MLA_KV_LORA], gkv_ref[...]).astype(BF16)
    kv = jnp.dot(ckv, wukv_ref[...], preferred_element_type=F32)
    k_rope = rope_m(proj[:, _OFF_KR:_OFF_KR + LANES]).astype(BF16)
    for p in range(N_PAIRS):
        lo = p * MLA_PAIR_W
        kc_ref[:, lo:lo + LANES] = kv[:, p * PAIR:(p + 1) * PAIR].astype(BF16)
        kc_ref[:, lo + LANES:lo + 2 * LANES] = k_rope
    v_ref[...] = kv[:, D_MLA_OUT:].astype(BF16)

    d_scale = LOG2_E * HEAD_DIM ** -0.5
    for p in range(N_DIL_HEADS // 2):
        lo = p * PAIR
        qd_ref[:, lo:lo + PAIR] = rope_d(proj[:, _OFF_QD + lo:_OFF_QD + lo + PAIR]) * d_scale
        kd_ref[:, lo:lo + PAIR] = rope_d(proj[:, _OFF_KD + lo:_OFF_KD + lo + PAIR])
    vd_ref[...] = proj[:, _OFF_VD:_OFF_VD + D_DIL_OUT]


def _proj(x3d, g, win, gq, wuq, gkv, wukv, tab):
    b, s, _ = x3d.shape

    def tok(width):
        return pl.BlockSpec((None, PROJ_ROWS, width), lambda si, bi: (bi, si, 0))

    def out(width, dtype):
        return jax.ShapeDtypeStruct((b, s, width), dtype)

    return pl.pallas_call(
        _proj_kernel,
        grid=(s // PROJ_ROWS, b),
        in_specs=[tok(D_MODEL), _resident((1, D_MODEL)), _resident((D_MODEL, _PROJ_W)),
                  _resident((1, MLA_Q_LORA)), _resident((MLA_Q_LORA, N_PAIRS * MLA_PAIR_W)),
                  _resident((1, MLA_KV_LORA)), _resident((MLA_KV_LORA, 2 * D_MLA_OUT)),
                  pl.BlockSpec((PROJ_ROWS, 6 * LANES), lambda si, bi: (si, 0))],
        out_specs=[tok(N_PAIRS * MLA_PAIR_W), tok(N_PAIRS * MLA_PAIR_W), tok(D_MLA_OUT),
                   tok(D_DIL_OUT), tok(D_DIL_OUT), tok(D_DIL_OUT)],
        out_shape=[out(N_PAIRS * MLA_PAIR_W, BF16), out(N_PAIRS * MLA_PAIR_W, BF16),
                   out(D_MLA_OUT, BF16), out(D_DIL_OUT, F32), out(D_DIL_OUT, F32),
                   out(D_DIL_OUT, F32)],
        compiler_params=pltpu.CompilerParams(
            dimension_semantics=("parallel", "parallel"),
            vmem_limit_bytes=V7X_VMEM_LIMIT_BYTES),
        name="proj",
    )(x3d, g, win, gq, wuq, gkv, wukv, tab)


def _mla_kernel(q_ref, kc_ref, v_ref, o_ref, s_ref):
    s_len = kc_ref.shape[0]
    n_tiles = s_len // MLA_KEY_TILE
    v = v_ref[...]
    first_v = lax.broadcasted_iota(jnp.int32, v.shape, 1) < MLA_V_DIM
    one = jnp.ones_like(v)
    v_heads = (jnp.where(first_v, v, one), jnp.where(first_v, one, v))
    lane_q = lax.broadcasted_iota(jnp.int32, (MLA_Q_ROWS, MLA_PAIR_W), 1)
    first_o = lax.broadcasted_iota(jnp.int32, (MLA_Q_ROWS, PAIR), 1) < MLA_V_DIM
    chains = [(c, h) for c in range(s_len // MLA_Q_ROWS) for h in range(2)]

    def masked_q(c, h):
        q = q_ref[c * MLA_Q_ROWS:(c + 1) * MLA_Q_ROWS, :]
        nope = (lane_q >= h * MLA_NOPE_DIM) & (lane_q < (h + 1) * MLA_NOPE_DIM)
        rope = ((lane_q >= LANES + h * MLA_ROPE_DIM)
                & (lane_q < LANES + (h + 1) * MLA_ROPE_DIM))
        return jnp.where(nope | rope, q, jnp.zeros_like(q))

    def score_tile(slot, qh, t, run_max):
        keys = slice(t * MLA_KEY_TILE, (t + 1) * MLA_KEY_TILE)
        s_t = lax.dot_general(qh, kc_ref[keys, :], (((1,), (1,)), ((), ())),
                              preferred_element_type=F32)
        s_ref[slot, :, keys] = s_t
        tile_max = functools.reduce(
            jnp.maximum, [s_t[:, i:i + LANES] for i in range(0, MLA_KEY_TILE, LANES)])
        return tile_max if run_max is None else jnp.maximum(run_max, tile_max)

    def pv_tile(slot, h, t, m, acc):
        keys = slice(t * MLA_KEY_TILE, (t + 1) * MLA_KEY_TILE)
        p_t = jnp.exp2((s_ref[slot, :, keys] - m).astype(BF16))
        d = jnp.dot(p_t, v_heads[h][keys, :], preferred_element_type=F32)
        return d if acc is None else acc + d

    qh = masked_q(*chains[0])
    run_max = None
    for t in range(n_tiles):
        run_max = score_tile(0, qh, t, run_max)
    m = jnp.max(run_max, axis=-1, keepdims=True)
    res = []
    for i, (c, h) in enumerate(chains):
        nxt = chains[i + 1] if i + 1 < len(chains) else None
        if nxt is not None:
            qh = masked_q(*nxt)
        run_max, acc = None, None
        for t in range(n_tiles):
            if nxt is not None:
                run_max = score_tile((i + 1) % 2, qh, t, run_max)
            acc = pv_tile(i % 2, h, t, m, acc)
        if nxt is not None:
            m = jnp.max(run_max, axis=-1, keepdims=True)
        res.append(acc)
        if h == 1:
            num = jnp.where(first_o, res[0], res[1])
            den = pltpu.roll(jnp.where(first_o, res[1], res[0]), MLA_V_DIM, 1)
            o_ref[c * MLA_Q_ROWS:(c + 1) * MLA_Q_ROWS, :] = (num / den).astype(o_ref.dtype)
            res = []


def _mla(q, kc, v):
    b, s, _ = q.shape
    return pl.pallas_call(
        _mla_kernel,
        grid=(b, N_PAIRS),
        in_specs=[pl.BlockSpec((None, s, MLA_PAIR_W), lambda bi, p: (bi, 0, p)),
                  pl.BlockSpec((None, s, MLA_PAIR_W), lambda bi, p: (bi, 0, p)),
                  pl.BlockSpec((None, s, PAIR), lambda bi, p: (bi, 0, p))],
        out_specs=pl.BlockSpec((None, s, PAIR), lambda bi, p: (bi, 0, p)),
        out_shape=jax.ShapeDtypeStruct((b, s, D_MLA_OUT), BF16),
        scratch_shapes=[pltpu.VMEM((2, MLA_Q_ROWS, s), F32)],
        compiler_params=pltpu.CompilerParams(
            dimension_semantics=("parallel", "parallel"),
            vmem_limit_bytes=V7X_VMEM_LIMIT_BYTES),
        name="mla",
    )(q, kc, v)


DIL_SPAN = Q_BLOCK + 2 * HALF_WINDOW
DIL_MID = DIL_PAIRS[1][1]
assert DIL_PAIRS[0][1] == 1 and DIL_PAIRS[2][1] == DIL_MID * DIL_MID


def _band_bias():
    row = np.arange(Q_BLOCK)[:, None]
    col = np.arange(DIL_SPAN)[None, :]
    kinds = [np.abs(col - row - off) <= HALF_WINDOW
             for off in (0, HALF_WINDOW, 2 * HALF_WINDOW)]
    kinds.append((np.abs(col - row) <= HALF_WINDOW) & (col < Q_BLOCK))
    bias = np.where(np.stack(kinds), 0.0, MASK_VALUE).astype(np.float32)
    return np.concatenate([bias, bias], axis=1)


def _band_scores(q_stack, k_span, bias):
    return lax.dot_general(q_stack, k_span, (((1,), (1,)), ((), ())),
                           preferred_element_type=F32) + bias


def _band_probs(s):
    m = jnp.max(s, axis=-1, keepdims=True)
    return jnp.exp2((s - m).astype(BF16)), m


def _band_finish(p, m, v_ext):
    res = jnp.dot(p, v_ext, preferred_element_type=F32)
    first = lax.broadcasted_iota(jnp.int32, (Q_BLOCK, PAIR), 1) < HEAD_DIM
    acc = jnp.where(first, res[:Q_BLOCK, :PAIR], res[Q_BLOCK:, :PAIR])
    l = jnp.where(first, res[:Q_BLOCK, PAIR:], res[Q_BLOCK:, PAIR:])
    m_lanes = jnp.where(first, jnp.broadcast_to(m[:Q_BLOCK], (Q_BLOCK, PAIR)),
                        jnp.broadcast_to(m[Q_BLOCK:], (Q_BLOCK, PAIR)))
    return acc, l, m_lanes


def _dil_kernel(q_ref, k_ref, v_ref, bias_ref, o_ref, mid_in_ref, nat_ref, mid_ref):
    s_len = q_ref.shape[0]
    mid_len = s_len // DIL_MID
    far_len = mid_len // DIL_MID
    first = lax.broadcasted_iota(jnp.int32, (1, PAIR), 1) < HEAD_DIM

    for t, ref in enumerate((q_ref, k_ref, v_ref)):
        for r in range(DIL_MID):
            mid_in_ref[t, r] = ref[pl.ds(r, mid_len, stride=DIL_MID), :]

    groups = []

    def write_near(j, vals):
        for t, val in enumerate(vals):
            nat_ref[t, j * Q_BLOCK:(j + 1) * Q_BLOCK, :] = val
    groups.append((lambda: (q_ref[...], k_ref[...], v_ref[...]), s_len, write_near))

    for r in range(DIL_MID):
        def write_mid(j, vals, r=r):
            for t, val in enumerate(vals):
                mid_ref[0, t, r, j * Q_BLOCK:(j + 1) * Q_BLOCK, :] = val
        groups.append((lambda r=r: tuple(mid_in_ref[t, r] for t in range(3)), mid_len, write_mid))

    for r in range(DIL_MID):
        for rr in range(DIL_MID):
            sl = pl.ds(rr, far_len, stride=DIL_MID)
            def write_far(j, vals, r=r, sl=sl):
                for t, val in enumerate(vals):
                    mid_ref[1, t, r, sl, :] = val
            groups.append((lambda r=r, sl=sl: tuple(mid_in_ref[t, r, sl, :] for t in range(3)),
                           far_len, write_far))

    blocks = [(g, j) for g, (_, sub_len, _) in enumerate(groups)
              for j in range(sub_len // Q_BLOCK)]
    operands = {}

    def scores(g, j):
        load, sub_len, _ = groups[g]
        if g not in operands:
            q, k, v = load()
            vb = v.astype(BF16)
            operands.clear()
            operands[g] = (jnp.where(first, q, 0.0).astype(BF16),
                           jnp.where(first, 0.0, q).astype(BF16),
                           k.astype(BF16), jnp.concatenate([vb, jnp.ones_like(vb)], axis=1))
        qa, qb, kb, v_ext = operands[g]
        span = min(DIL_SPAN, sub_len)
        q0 = j * Q_BLOCK
        k0 = min(max(q0 - HALF_WINDOW, 0), sub_len - span)
        kind = 3 if span < DIL_SPAN else (q0 - k0) // HALF_WINDOW
        q_stack = jnp.concatenate([qa[q0:q0 + Q_BLOCK], qb[q0:q0 + Q_BLOCK]], axis=0)
        return (_band_scores(q_stack, kb[k0:k0 + span], bias_ref[kind][:, :span]),
                v_ext[k0:k0 + span])

    n = len(blocks)
    sc = {i: scores(*blocks[i]) for i in range(min(2, n))}
    pr = {0: _band_probs(sc[0][0])}
    for i, (g, j) in enumerate(blocks):
        if i + 2 < n:
            sc[i + 2] = scores(*blocks[i + 2])
        if i + 1 < n:
            pr[i + 1] = _band_probs(sc[i + 1][0])
        p, m = pr.pop(i)
        groups[g][2](j, _band_finish(p, m, sc.pop(i)[1]))

    for r in range(DIL_MID):
        nat = pl.ds(r, mid_len, stride=DIL_MID)
        accs = (nat_ref[0, nat, :], mid_ref[0, 0, r], mid_ref[1, 0, r])
        ls = (nat_ref[1, nat, :], mid_ref[0, 1, r], mid_ref[1, 1, r])
        ms = (nat_ref[2, nat, :], mid_ref[0, 2, r], mid_ref[1, 2, r])
        top = jnp.maximum(jnp.maximum(ms[0], ms[1]), ms[2])
        es = [jnp.exp2(m - top) for m in ms]
        num = es[0] * accs[0] + es[1] * accs[1] + es[2] * accs[2]
        den = es[0] * ls[0] + es[1] * ls[1] + es[2] * ls[2]
        o_ref[nat, :] = num / den


def _dilated(qd, kd, vd):
    b, s, _ = qd.shape
    blk = pl.BlockSpec((None, s, PAIR), lambda bi, p: (bi, 0, p))
    bias = jnp.asarray(_band_bias())
    return pl.pallas_call(
        _dil_kernel,
        grid=(b, N_DIL_HEADS // 2),
        in_specs=[blk, blk, blk, _resident(bias.shape)],
        out_specs=blk,
        out_shape=jax.ShapeDtypeStruct((b, s, D_DIL_OUT), F32),
        scratch_shapes=[pltpu.VMEM((3, DIL_MID, s // DIL_MID, PAIR), F32),
                        pltpu.VMEM((3, s, PAIR), F32),
                        pltpu.VMEM((2, 3, DIL_MID, s // DIL_MID, PAIR), F32)],
        compiler_params=pltpu.CompilerParams(
            dimension_semantics=("parallel", "parallel"),
            vmem_limit_bytes=V7X_VMEM_LIMIT_BYTES),
        name="dilated",
    )(qd, kd, vd, bias)


def _rope_lane_tables(seq):
    lane = np.arange(LANES)

    def tables(rot_dim, unit, rotary_lanes):
        half = rot_dim // 2
        pos_in_unit = lane % unit
        rot = (pos_in_unit < rot_dim) & (lane < rotary_lanes)
        first = rot & (pos_in_unit < half)
        second = rot & (pos_in_unit >= half)
        inv_freq = ROPE_THETA ** (-jnp.arange(half, dtype=F32) * (2.0 / rot_dim))
        ang = jnp.arange(seq, dtype=F32)[:, None] * inv_freq[pos_in_unit % half][None, :]
        cos, sin = jnp.cos(ang), jnp.sin(ang)
        return [jnp.where(rot, cos, 1.0), jnp.where(first, -sin, 0.0), jnp.where(second, sin, 0.0)]

    return jnp.concatenate(tables(MLA_ROPE_DIM, MLA_ROPE_DIM, 2 * MLA_ROPE_DIM)
                           + tables(PARTIAL_ROT_DIM, HEAD_DIM, LANES), axis=-1)


def _reorder_w_in(w_in):
    c_q = w_in[:, :MLA_Q_LORA]
    c_kv = w_in[:, MLA_Q_LORA:MLA_Q_LORA + MLA_KV_LORA]
    k_r = w_in[:, MLA_Q_LORA + MLA_KV_LORA:MLA_Q_LORA + MLA_KV_LORA + MLA_ROPE_DIM]
    rest = w_in[:, MLA_Q_LORA + MLA_KV_LORA + MLA_ROPE_DIM:]
    pad = jnp.zeros((w_in.shape[0], LANES - 2 * MLA_ROPE_DIM), w_in.dtype)
    return jnp.concatenate([c_q, c_kv, k_r, k_r, pad, rest], axis=-1)


def _reorder_w_uq(w_uq):
    w = w_uq.reshape(MLA_Q_LORA, N_PAIRS, 2, MLA_NOPE_DIM + MLA_ROPE_DIM)
    nope = w[..., :MLA_NOPE_DIM].reshape(MLA_Q_LORA, N_PAIRS, 2 * MLA_NOPE_DIM)
    rope = w[..., MLA_NOPE_DIM:].reshape(MLA_Q_LORA, N_PAIRS, 2 * MLA_ROPE_DIM)
    pad = jnp.zeros((MLA_Q_LORA, N_PAIRS, LANES - 2 * MLA_ROPE_DIM), w_uq.dtype)
    return jnp.concatenate([nope, rope, pad], axis=-1).reshape(MLA_Q_LORA, N_PAIRS * MLA_PAIR_W)


def _reorder_w_ukv(w_ukv):
    w = w_ukv.reshape(MLA_KV_LORA, N_MLA_HEADS, MLA_NOPE_DIM + MLA_V_DIM)
    k_nope = w[..., :MLA_NOPE_DIM].reshape(MLA_KV_LORA, D_MLA_OUT)
    v = w[..., MLA_NOPE_DIM:].reshape(MLA_KV_LORA, D_MLA_OUT)
    return jnp.concatenate([k_nope, v], axis=-1)


def kernel(x, ffn1_norm, ffn1_w_gate, ffn1_w_up, ffn1_w_down, mix_norm, w_in, mla_q_norm, mla_w_uq, mla_kv_norm, mla_w_ukv, mla_out_norm, dil_out_norm, w_out, ffn2_norm, ffn2_w_gate, ffn2_w_up, ffn2_w_down, final_norm):
    b, s, d = x.shape
    assert ffn1_norm.shape[0] == 1, "single-layer block: the final norm is fused into its last FFN"
    assert d == D_MODEL and s % MLA_Q_ROWS == 0 and s % (Q_BLOCK * DIL_PAIRS[-1][1]) == 0
    tab = _rope_lane_tables(s)
    x1 = _ffn1(x.reshape(b * s, d), ffn1_norm[0][None], ffn1_w_gate[0].astype(BF16),
               ffn1_w_up[0].astype(BF16), ffn1_w_down[0].astype(BF16))
    q, kc, v, qd, kd, vd = _proj(
        x1.reshape(b, s, d), mix_norm[0][None], _reorder_w_in(w_in[0]).astype(BF16),
        mla_q_norm[0][None], _reorder_w_uq(mla_w_uq[0]).astype(BF16),
        mla_kv_norm[0][None], _reorder_w_ukv(mla_w_ukv[0]).astype(BF16), tab)
    o_mla = _mla(q, kc, v).reshape(b * s, D_MLA_OUT)
    o_dil = _dilated(qd, kd, vd).reshape(b * s, D_DIL_OUT)
    y = _ffn2(x1, o_mla, o_dil, mla_out_norm[0][None], dil_out_norm[0][None],
              w_out[0].astype(BF16), ffn2_norm[0][None],
              ffn2_w_gate[0].astype(BF16), ffn2_w_up[0].astype(BF16),
              ffn2_w_down[0].astype(BF16), final_norm[None])
    return y.reshape(b, s, d)
```

```python
import functools

import numpy as np
import jax
import jax.numpy as jnp
from jax import lax
from jax.experimental import pallas as pl
from jax.experimental.pallas import tpu as pltpu

F32 = jnp.float32
BF16 = jnp.bfloat16

D_MODEL = 1024
HEAD_DIM = 64
N_MLA_HEADS = 8
N_DIL_HEADS = 8
MLA_NOPE_DIM = 64
MLA_ROPE_DIM = 32
MLA_V_DIM = 64
MLA_Q_LORA = 384
MLA_KV_LORA = 256
D_MLA_OUT = N_MLA_HEADS * MLA_V_DIM
D_DIL_OUT = N_DIL_HEADS * HEAD_DIM
DIL_PAIRS = ((128, 1), (512, 4), (2048, 16))
PARTIAL_ROT_DIM = 16
ROPE_THETA = 500000.0
D_FF = 2816
Q_BLOCK = 128
NORM_EPS = 1e-6
MASK_VALUE = -1e30
HALF_WINDOW = 64
assert all(w // (2 * d) == HALF_WINDOW for w, d in DIL_PAIRS)
LOG2_E = 1.4426950408889634

LANES = 128
V7X_VMEM_LIMIT_BYTES = 56 * 1024 * 1024

FFN_ROWS = 1024
FFN_HIDDEN_CHUNK = 1024
PROJ_ROWS = 1024
MLA_Q_ROWS = 512
MLA_KEY_TILE = 256
PAIR = 2 * HEAD_DIM
N_PAIRS = N_MLA_HEADS // 2
MLA_PAIR_W = 2 * LANES
HALF_LANES = LANES // 2
MLA_ROPE_HALF = MLA_ROPE_DIM // 2
DIL_ROT_HALF = PARTIAL_ROT_DIM // 2
DIL_HEAD_HALF = HEAD_DIM // 2

_OFF_CQ = 0
_OFF_CKV = _OFF_CQ + MLA_Q_LORA
_OFF_KR = _OFF_CKV + MLA_KV_LORA
_OFF_QD = _OFF_KR + LANES
_OFF_KD = _OFF_QD + D_DIL_OUT
_OFF_VD = _OFF_KD + D_DIL_OUT
_PROJ_W = _OFF_VD + D_DIL_OUT


def _rms(x, g):
    ms = jnp.mean(x * x, axis=-1, keepdims=True)
    return x * lax.rsqrt(ms + NORM_EPS) * g


def _rope(x, c, s):
    return x * c + pltpu.roll(x, LANES // 2, 1) * s


def _swiglu_residual(x, g_ref, wg_ref, wu_ref, wd_ref):
    h = _rms(x, g_ref[...]).astype(BF16)
    out = None
    for lo in range(0, D_FF, FFN_HIDDEN_CHUNK):
        hi = min(lo + FFN_HIDDEN_CHUNK, D_FF)
        gate = jnp.dot(h, wg_ref[:, lo:hi], preferred_element_type=F32)
        up = jnp.dot(h, wu_ref[:, lo:hi], preferred_element_type=F32)
        act = (gate * jax.nn.sigmoid(gate) * up).astype(BF16)
        part = jnp.dot(act, wd_ref[lo:hi, :], preferred_element_type=F32)
        out = part if out is None else out + part
    return x + 0.5 * out


def _ffn1_kernel(x_ref, g_ref, wg_ref, wu_ref, wd_ref, o_ref):
    o_ref[...] = _swiglu_residual(x_ref[...], g_ref, wg_ref, wu_ref, wd_ref)


def _ffn2_kernel(x_ref, om_ref, od_ref, gm_ref, gd_ref, wo_ref,
                 g_ref, wg_ref, wu_ref, wd_ref, gf_ref, o_ref):
    nm = _rms(om_ref[...].astype(F32), gm_ref[...]).astype(BF16)
    nd = _rms(od_ref[...].astype(F32), gd_ref[...]).astype(BF16)
    x = (x_ref[...]
         + jnp.dot(nm, wo_ref[:D_MLA_OUT, :], preferred_element_type=F32)
         + jnp.dot(nd, wo_ref[D_MLA_OUT:, :], preferred_element_type=F32))
    o_ref[...] = _rms(_swiglu_residual(x, g_ref, wg_ref, wu_ref, wd_ref), gf_ref[...])


def _resident(shape):
    return pl.BlockSpec(shape, lambda *_: (0,) * len(shape), pipeline_mode=pl.Buffered(1))


def _ffn1(x2d, g, wg, wu, wd):
    n = x2d.shape[0]
    row = pl.BlockSpec((FFN_ROWS, D_MODEL), lambda i: (i, 0))
    return pl.pallas_call(
        _ffn1_kernel,
        grid=(n // FFN_ROWS,),
        in_specs=[row, _resident((1, D_MODEL)), _resident((D_MODEL, D_FF)),
                  _resident((D_MODEL, D_FF)), _resident((D_FF, D_MODEL))],
        out_specs=row,
        out_shape=jax.ShapeDtypeStruct((n, D_MODEL), F32),
        compiler_params=pltpu.CompilerParams(
            dimension_semantics=("parallel",), vmem_limit_bytes=V7X_VMEM_LIMIT_BYTES),
        name="ffn1",
    )(x2d, g, wg, wu, wd)


def _ffn2(x2d, om, od, gm, gd, wo, g, wg, wu, wd, gf):
    n = x2d.shape[0]
    row = pl.BlockSpec((FFN_ROWS, D_MODEL), lambda i: (i, 0))
    half_row = pl.BlockSpec((FFN_ROWS, D_MLA_OUT), lambda i: (i, 0))
    return pl.pallas_call(
        _ffn2_kernel,
        grid=(n // FFN_ROWS,),
        in_specs=[row, half_row, half_row,
                  _resident((1, D_MLA_OUT)), _resident((1, D_DIL_OUT)),
                  _resident((D_MLA_OUT + D_DIL_OUT, D_MODEL)),
                  _resident((1, D_MODEL)), _resident((D_MODEL, D_FF)),
                  _resident((D_MODEL, D_FF)), _resident((D_FF, D_MODEL)),
                  _resident((1, D_MODEL))],
        out_specs=row,
        out_shape=jax.ShapeDtypeStruct((n, D_MODEL), F32),
        compiler_params=pltpu.CompilerParams(
            dimension_semantics=("parallel",), vmem_limit_bytes=V7X_VMEM_LIMIT_BYTES),
        name="ffn2",
    )(x2d, om, od, gm, gd, wo, g, wg, wu, wd, gf)


def _proj_kernel(x_ref, g_ref, win_ref, gq_ref, wuq_ref, gkv_ref, wukv_ref, tab_ref,
                 q_ref, kc_ref, v_ref, qd_ref, kd_ref, vd_ref):
    h = _rms(x_ref[...], g_ref[...]).astype(BF16)
    lat = jnp.dot(h, win_ref[:, :_OFF_QD], preferred_element_type=F32)
    dil = jnp.dot(h, win_ref[:, _OFF_QD:], preferred_element_type=F32)

    cm, sm, cd, sd = (tab_ref[:, i * LANES:(i + 1) * LANES] for i in range(4))
    rope_m = functools.partial(_rope, c=cm, s=sm)
    rope_d = functools.partial(_rope, c=cd, s=sd)

    cq = _rms(lat[:, _OFF_CQ:_OFF_CQ + MLA_Q_LORA], gq_ref[...]).astype(BF16)
    ckv = _rms(lat[:, _OFF_CKV:_OFF_CKV + MLA_KV_LORA], gkv_ref[...]).astype(BF16)
    q = jnp.dot(cq, wuq_ref[...], preferred_element_type=F32)

    d_scale = LOG2_E * HEAD_DIM ** -0.5
    for p in range(N_DIL_HEADS // 2):
        lo = p * PAIR
        qd_ref[:, lo:lo + PAIR] = rope_d(dil[:, lo:lo + PAIR]) * d_scale

    kv = jnp.dot(ckv, wukv_ref[...], preferred_element_type=F32)

    for p in range(N_DIL_HEADS // 2):
        lo = D_DIL_OUT + p * PAIR
        kd_ref[:, p * PAIR:(p + 1) * PAIR] = rope_d(dil[:, lo:lo + PAIR])
    vd_ref[...] = dil[:, 2 * D_DIL_OUT:]

    q_scale = LOG2_E * (MLA_NOPE_DIM + MLA_ROPE_DIM) ** -0.5
    for p in range(N_PAIRS):
        lo = p * MLA_PAIR_W
        q_ref[:, lo:lo + LANES] = (q[:, lo:lo + LANES] * q_scale).astype(BF16)
        q_ref[:, lo + LANES:lo + 2 * LANES] = (
            rope_m(q[:, lo + LANES:lo + 2 * LANES]) * q_scale).astype(BF16)

    k_rope = rope_m(lat[:, _OFF_KR:_OFF_KR + LANES]).astype(BF16)
    for p in range(N_PAIRS):
        lo = p * MLA_PAIR_W
        kc_ref[:, lo:lo + LANES] = kv[:, p * PAIR:(p + 1) * PAIR].astype(BF16)
        kc_ref[:, lo + LANES:lo + 2 * LANES] = k_rope
    v_ref[...] = kv[:, D_MLA_OUT:].astype(BF16)


def _proj(x3d, g, win, gq, wuq, gkv, wukv, tab):
    b, s, _ = x3d.shape

    def tok(width):
        return pl.BlockSpec((None, PROJ_ROWS, width), lambda si, bi: (bi, si, 0))

    def out(width, dtype):
        return jax.ShapeDtypeStruct((b, s, width), dtype)

    return pl.pallas_call(
        _proj_kernel,
        grid=(s // PROJ_ROWS, b),
        in_specs=[tok(D_MODEL), _resident((1, D_MODEL)), _resident((D_MODEL, _PROJ_W)),
                  _resident((1, MLA_Q_LORA)), _resident((MLA_Q_LORA, N_PAIRS * MLA_PAIR_W)),
                  _resident((1, MLA_KV_LORA)), _resident((MLA_KV_LORA, 2 * D_MLA_OUT)),
                  pl.BlockSpec((PROJ_ROWS, 4 * LANES), lambda si, bi: (si, 0))],
        out_specs=[tok(N_PAIRS * MLA_PAIR_W), tok(N_PAIRS * MLA_PAIR_W), tok(D_MLA_OUT),
                   tok(D_DIL_OUT), tok(D_DIL_OUT), tok(D_DIL_OUT)],
        out_shape=[out(N_PAIRS * MLA_PAIR_W, BF16), out(N_PAIRS * MLA_PAIR_W, BF16),
                   out(D_MLA_OUT, BF16), out(D_DIL_OUT, F32), out(D_DIL_OUT, F32),
                   out(D_DIL_OUT, F32)],
        compiler_params=pltpu.CompilerParams(
            dimension_semantics=("parallel", "parallel"),
            vmem_limit_bytes=V7X_VMEM_LIMIT_BYTES),
        name="proj",
    )(x3d, g, win, gq, wuq, gkv, wukv, tab)


def _mla_kernel(q_ref, kc_ref, v_ref, o_ref, s_ref):
    s_len = kc_ref.shape[0]
    n_tiles = s_len // MLA_KEY_TILE
    v = v_ref[...]
    first_v = lax.broadcasted_iota(jnp.int32, v.shape, 1) < MLA_V_DIM
    one = jnp.ones_like(v)
    v_heads = (jnp.where(first_v, v, one), jnp.where(first_v, one, v))
    lane_q = lax.broadcasted_iota(jnp.int32, (MLA_Q_ROWS, MLA_PAIR_W), 1)
    first_o = lax.broadcasted_iota(jnp.int32, (MLA_Q_ROWS, PAIR), 1) < MLA_V_DIM
    chains = [(c, h) for c in range(s_len // MLA_Q_ROWS) for h in range(2)]

    def masked_q(c, h):
        q = q_ref[c * MLA_Q_ROWS:(c + 1) * MLA_Q_ROWS, :]
        nope = (lane_q >= h * MLA_NOPE_DIM) & (lane_q < (h + 1) * MLA_NOPE_DIM)
        in_half = lane_q & (HALF_LANES - 1)
        rope = ((lane_q >= LANES) & (in_half >= h * MLA_ROPE_HALF)
                & (in_half < (h + 1) * MLA_ROPE_HALF))
        return jnp.where(nope | rope, q, jnp.zeros_like(q))

    def score_tile(slot, qh, t, run_max):
        keys = slice(t * MLA_KEY_TILE, (t + 1) * MLA_KEY_TILE)
        s_t = lax.dot_general(qh, kc_ref[keys, :], (((1,), (1,)), ((), ())),
                              preferred_element_type=F32)
        s_ref[slot, :, keys] = s_t
        tile_max = functools.reduce(
            jnp.maximum, [s_t[:, i:i + LANES] for i in range(0, MLA_KEY_TILE, LANES)])
        return tile_max if run_max is None else jnp.maximum(run_max, tile_max)

    def pv_tile(slot, h, t, m, acc):
        keys = slice(t * MLA_KEY_TILE, (t + 1) * MLA_KEY_TILE)
        p_t = jnp.exp2((s_ref[slot, :, keys] - m).astype(BF16))
        d = jnp.dot(p_t, v_heads[h][keys, :], preferred_element_type=F32)
        return d if acc is None else acc + d

    qh = masked_q(*chains[0])
    run_max = None
    for t in range(n_tiles):
        run_max = score_tile(0, qh, t, run_max)
    m = jnp.max(run_max, axis=-1, keepdims=True)
    res = []
    for i, (c, h) in enumerate(chains):
        nxt = chains[i + 1] if i + 1 < len(chains) else None
        if nxt is not None:
            qh = masked_q(*nxt)
        run_max, acc = None, None
        for t in range(n_tiles):
            if nxt is not None:
                run_max = score_tile((i + 1) % 2, qh, t, run_max)
            acc = pv_tile(i % 2, h, t, m, acc)
        if nxt is not None:
            m = jnp.max(run_max, axis=-1, keepdims=True)
        res.append(acc)
        if h == 1:
            num = jnp.where(first_o, res[0], res[1])
            den = pltpu.roll(jnp.where(first_o, res[1], res[0]), MLA_V_DIM, 1)
            o_ref[c * MLA_Q_ROWS:(c + 1) * MLA_Q_ROWS, :] = (num / den).astype(o_ref.dtype)
            res = []


def _mla(q, kc, v):
    b, s, _ = q.shape
    return pl.pallas_call(
        _mla_kernel,
        grid=(b, N_PAIRS),
        in_specs=[pl.BlockSpec((None, s, MLA_PAIR_W), lambda bi, p: (bi, 0, p)),
                  pl.BlockSpec((None, s, MLA_PAIR_W), lambda bi, p: (bi, 0, p)),
                  pl.BlockSpec((None, s, PAIR), lambda bi, p: (bi, 0, p))],
        out_specs=pl.BlockSpec((None, s, PAIR), lambda bi, p: (bi, 0, p)),
        out_shape=jax.ShapeDtypeStruct((b, s, D_MLA_OUT), BF16),
        scratch_shapes=[pltpu.VMEM((2, MLA_Q_ROWS, s), F32)],
        compiler_params=pltpu.CompilerParams(
            dimension_semantics=("parallel", "parallel"),
            vmem_limit_bytes=V7X_VMEM_LIMIT_BYTES),
        name="mla",
    )(q, kc, v)


DIL_SPAN = Q_BLOCK + 2 * HALF_WINDOW
DIL_MID = DIL_PAIRS[1][1]
assert DIL_PAIRS[0][1] == 1 and DIL_PAIRS[2][1] == DIL_MID * DIL_MID


def _band_bias():
    row = np.arange(Q_BLOCK)[:, None]
    col = np.arange(DIL_SPAN)[None, :]
    kinds = [np.abs(col - row - off) <= HALF_WINDOW
             for off in (0, HALF_WINDOW, 2 * HALF_WINDOW)]
    kinds.append((np.abs(col - row) <= HALF_WINDOW) & (col < Q_BLOCK))
    bias = np.where(np.stack(kinds), 0.0, MASK_VALUE).astype(np.float32)
    return np.concatenate([bias, bias], axis=1)


def _band_scores(q_stack, k_span, bias):
    return lax.dot_general(q_stack, k_span, (((1,), (1,)), ((), ())),
                           preferred_element_type=F32) + bias


def _band_probs(s):
    m = jnp.max(s, axis=-1, keepdims=True)
    return jnp.exp2((s - m).astype(BF16)), m


def _band_finish(p, m, v_ext):
    res = jnp.dot(p, v_ext, preferred_element_type=F32)
    first = lax.broadcasted_iota(jnp.int32, (Q_BLOCK, PAIR), 1) < HEAD_DIM
    acc = jnp.where(first, res[:Q_BLOCK, :PAIR], res[Q_BLOCK:, :PAIR])
    l = jnp.where(first, res[:Q_BLOCK, PAIR:], res[Q_BLOCK:, PAIR:])
    m_lanes = jnp.where(first, jnp.broadcast_to(m[:Q_BLOCK], (Q_BLOCK, PAIR)),
                        jnp.broadcast_to(m[Q_BLOCK:], (Q_BLOCK, PAIR)))
    return acc, l, m_lanes


def _dil_kernel(q_ref, k_ref, v_ref, bias_ref, o_ref, mid_in_ref, nat_ref, mid_ref):
    s_len = q_ref.shape[0]
    mid_len = s_len // DIL_MID
    far_len = mid_len // DIL_MID
    lane = lax.broadcasted_iota(jnp.int32, (1, PAIR), 1)
    first = (lane & (HALF_LANES - 1)) < DIL_HEAD_HALF

    for t, ref in enumerate((q_ref, k_ref, v_ref)):
        for r in range(DIL_MID):
            mid_in_ref[t, r] = ref[pl.ds(r, mid_len, stride=DIL_MID), :]

    groups = []

    def write_near(j, vals):
        for t, val in enumerate(vals):
            nat_ref[t, j * Q_BLOCK:(j + 1) * Q_BLOCK, :] = val
    groups.append((lambda: (q_ref[...], k_ref[...], v_ref[...]), s_len, write_near))

    for r in range(DIL_MID):
        def write_mid(j, vals, r=r):
            for t, val in enumerate(vals):
                mid_ref[0, t, r, j * Q_BLOCK:(j + 1) * Q_BLOCK, :] = val
        groups.append((lambda r=r: tuple(mid_in_ref[t, r] for t in range(3)), mid_len, write_mid))

    for r in range(DIL_MID):
        for rr in range(DIL_MID):
            sl = pl.ds(rr, far_len, stride=DIL_MID)
            def write_far(j, vals, r=r, sl=sl):
                for t, val in enumerate(vals):
                    mid_ref[1, t, r, sl, :] = val
            groups.append((lambda r=r, sl=sl: tuple(mid_in_ref[t, r, sl, :] for t in range(3)),
                           far_len, write_far))

    blocks = [(g, j) for g, (_, sub_len, _) in enumerate(groups)
              for j in range(sub_len // Q_BLOCK)]
    operands = {}

    def scores(g, j):
        load, sub_len, _ = groups[g]
        if g not in operands:
            q, k, v = load()
            vb = v.astype(BF16)
            operands.clear()
            operands[g] = (jnp.where(first, q, 0.0).astype(BF16),
                           jnp.where(first, 0.0, q).astype(BF16),
                           k.astype(BF16), jnp.concatenate([vb, jnp.ones_like(vb)], axis=1))
        qa, qb, kb, v_ext = operands[g]
        span = min(DIL_SPAN, sub_len)
        q0 = j * Q_BLOCK
        k0 = min(max(q0 - HALF_WINDOW, 0), sub_len - span)
        kind = 3 if span < DIL_SPAN else (q0 - k0) // HALF_WINDOW
        q_stack = jnp.concatenate([qa[q0:q0 + Q_BLOCK], qb[q0:q0 + Q_BLOCK]], axis=0)
        return (_band_scores(q_stack, kb[k0:k0 + span], bias_ref[kind][:, :span]),
                v_ext[k0:k0 + span])

    n = len(blocks)
    sc = {i: scores(*blocks[i]) for i in range(min(2, n))}
    pr = {0: _band_probs(sc[0][0])}
    for i, (g, j) in enumerate(blocks):
        if i + 2 < n:
            sc[i + 2] = scores(*blocks[i + 2])
        if i + 1 < n:
            pr[i + 1] = _band_probs(sc[i + 1][0])
        p, m = pr.pop(i)
        groups[g][2](j, _band_finish(p, m, sc.pop(i)[1]))

    for r in range(DIL_MID):
        nat = pl.ds(r, mid_len, stride=DIL_MID)
        accs = (nat_ref[0, nat, :], mid_ref[0, 0, r], mid_ref[1, 0, r])
        ls = (nat_ref[1, nat, :], mid_ref[0, 1, r], mid_ref[1, 1, r])
        ms = (nat_ref[2, nat, :], mid_ref[0, 2, r], mid_ref[1, 2, r])
        top = jnp.maximum(jnp.maximum(ms[0], ms[1]), ms[2])
        es = [jnp.exp2(m - top) for m in ms]
        num = es[0] * accs[0] + es[1] * accs[1] + es[2] * accs[2]
        den = es[0] * ls[0] + es[1] * ls[1] + es[2] * ls[2]
        o_ref[nat, :] = num / den


def _dilated(qd, kd, vd):
    b, s, _ = qd.shape
    blk = pl.BlockSpec((None, s, PAIR), lambda bi, p: (bi, 0, p))
    bias = jnp.asarray(_band_bias())
    return pl.pallas_call(
        _dil_kernel,
        grid=(b, N_DIL_HEADS // 2),
        in_specs=[blk, blk, blk, _resident(bias.shape)],
        out_specs=blk,
        out_shape=jax.ShapeDtypeStruct((b, s, D_DIL_OUT), F32),
        scratch_shapes=[pltpu.VMEM((3, DIL_MID, s // DIL_MID, PAIR), F32),
                        pltpu.VMEM((3, s, PAIR), F32),
                        pltpu.VMEM((2, 3, DIL_MID, s // DIL_MID, PAIR), F32)],
        compiler_params=pltpu.CompilerParams(
            dimension_semantics=("parallel", "parallel"),
            vmem_limit_bytes=V7X_VMEM_LIMIT_BYTES),
        name="dilated",
    )(qd, kd, vd, bias)


def _rope_lane_tables(seq):
    lane = np.arange(LANES)
    first = lane < HALF_LANES

    def tables(rot_dim, rotary_lanes, period):
        half = rot_dim // 2
        u = lane % period
        rot = u < rotary_lanes
        inv_freq = ROPE_THETA ** (-jnp.arange(half, dtype=F32) * (2.0 / rot_dim))
        ang = jnp.arange(seq, dtype=F32)[:, None] * inv_freq[u % half][None, :]
        cos, sin = jnp.cos(ang), jnp.sin(ang)
        return [jnp.where(rot, cos, 1.0),
                jnp.where(rot & first, -sin, jnp.where(rot & ~first, sin, 0.0))]

    return jnp.concatenate(tables(MLA_ROPE_DIM, 2 * MLA_ROPE_HALF, HALF_LANES)
                           + tables(PARTIAL_ROT_DIM, DIL_ROT_HALF, DIL_HEAD_HALF), axis=-1)


def _reorder_w_in(w_in):
    c_q = w_in[:, :MLA_Q_LORA]
    c_kv = w_in[:, MLA_Q_LORA:MLA_Q_LORA + MLA_KV_LORA]
    k_r = w_in[:, MLA_Q_LORA + MLA_KV_LORA:MLA_Q_LORA + MLA_KV_LORA + MLA_ROPE_DIM]
    rest = w_in[:, MLA_Q_LORA + MLA_KV_LORA + MLA_ROPE_DIM:]
    d_in = w_in.shape[0]
    pad = jnp.zeros((d_in, HALF_LANES - 2 * MLA_ROPE_HALF), w_in.dtype)
    k_x1, k_x2 = k_r[:, :MLA_ROPE_HALF], k_r[:, MLA_ROPE_HALF:]
    k_block = jnp.concatenate([k_x1, k_x1, pad, k_x2, k_x2, pad], axis=-1)
    q_d, k_d, v_d = (rest[:, i * D_DIL_OUT:(i + 1) * D_DIL_OUT] for i in range(3))
    return jnp.concatenate([c_q, c_kv, k_block, _dil_rotary_layout(q_d),
                            _dil_rotary_layout(k_d), v_d], axis=-1)


def _dil_rotary_layout(w):
    w = w.reshape(w.shape[0], N_DIL_HEADS // 2, 2, HEAD_DIM)
    n_pass = DIL_HEAD_HALF - DIL_ROT_HALF
    x1 = w[..., :DIL_ROT_HALF]
    x2 = w[..., DIL_ROT_HALF:PARTIAL_ROT_DIM]
    p1 = w[..., PARTIAL_ROT_DIM:PARTIAL_ROT_DIM + n_pass]
    p2 = w[..., PARTIAL_ROT_DIM + n_pass:]
    halves = [jnp.concatenate([x[:, :, 0], p[:, :, 0], x[:, :, 1], p[:, :, 1]], axis=-1)
              for x, p in ((x1, p1), (x2, p2))]
    return jnp.concatenate(halves, axis=-1).reshape(w.shape[0], D_DIL_OUT)


def _reorder_w_uq(w_uq):
    w = w_uq.reshape(MLA_Q_LORA, N_PAIRS, 2, MLA_NOPE_DIM + MLA_ROPE_DIM)
    nope = w[..., :MLA_NOPE_DIM].reshape(MLA_Q_LORA, N_PAIRS, 2 * MLA_NOPE_DIM)
    rope = w[..., MLA_NOPE_DIM:]
    pad = jnp.zeros((MLA_Q_LORA, N_PAIRS, HALF_LANES - 2 * MLA_ROPE_HALF), w_uq.dtype)
    x1, x2 = rope[..., :MLA_ROPE_HALF], rope[..., MLA_ROPE_HALF:]
    block = [x1[:, :, 0], x1[:, :, 1], pad, x2[:, :, 0], x2[:, :, 1], pad]
    return jnp.concatenate([nope] + block, axis=-1).reshape(MLA_Q_LORA, N_PAIRS * MLA_PAIR_W)


def _reorder_w_ukv(w_ukv):
    w = w_ukv.reshape(MLA_KV_LORA, N_MLA_HEADS, MLA_NOPE_DIM + MLA_V_DIM)
    k_nope = w[..., :MLA_NOPE_DIM].reshape(MLA_KV_LORA, D_MLA_OUT)
    v = w[..., MLA_NOPE_DIM:].reshape(MLA_KV_LORA, D_MLA_OUT)
    return jnp.concatenate([k_nope, v], axis=-1)


def kernel(x, ffn1_norm, ffn1_w_gate, ffn1_w_up, ffn1_w_down, mix_norm, w_in, mla_q_norm, mla_w_uq, mla_kv_norm, mla_w_ukv, mla_out_norm, dil_out_norm, w_out, ffn2_norm, ffn2_w_gate, ffn2_w_up, ffn2_w_down, final_norm):
    b, s, d = x.shape
    assert ffn1_norm.shape[0] == 1, "single-layer block: the final norm is fused into its last FFN"
    assert d == D_MODEL and s % MLA_Q_ROWS == 0 and s % (Q_BLOCK * DIL_PAIRS[-1][1]) == 0
    tab = _rope_lane_tables(s)
    x1 = _ffn1(x.reshape(b * s, d), ffn1_norm[0][None], ffn1_w_gate[0].astype(BF16),
               ffn1_w_up[0].astype(BF16), ffn1_w_down[0].astype(BF16))
    q, kc, v, qd, kd, vd = _proj(
        x1.reshape(b, s, d), mix_norm[0][None], _reorder_w_in(w_in[0]).astype(BF16),
        mla_q_norm[0][None], _reorder_w_uq(mla_w_uq[0]).astype(BF16),
        mla_kv_norm[0][None], _reorder_w_ukv(mla_w_ukv[0]).astype(BF16), tab)
    o_mla = _mla(q, kc, v).reshape(b * s, D_MLA_OUT)
    o_dil = _dilated(qd, kd, vd).reshape(b * s, D_DIL_OUT)
    y = _ffn2(x1, o_mla, o_dil, mla_out_norm[0][None], dil_out_norm[0][None],
              w_out[0].astype(BF16), ffn2_norm[0][None],
              ffn2_w_gate[0].astype(BF16), ffn2_w_up[0].astype(BF16),
              ffn2_w_down[0].astype(BF16), final_norm[None])
    return y.reshape(b, s, d)
```

```python
import functools

import numpy as np
import jax
import jax.numpy as jnp
from jax import lax
from jax.experimental import pallas as pl
from jax.experimental.pallas import tpu as pltpu

F32 = jnp.float32
BF16 = jnp.bfloat16

D_MODEL = 1024
HEAD_DIM = 64
N_MLA_HEADS = 8
N_DIL_HEADS = 8
MLA_NOPE_DIM = 64
MLA_ROPE_DIM = 32
MLA_V_DIM = 64
MLA_Q_LORA = 384
MLA_KV_LORA = 256
D_MLA_OUT = N_MLA_HEADS * MLA_V_DIM
D_DIL_OUT = N_DIL_HEADS * HEAD_DIM
DIL_PAIRS = ((128, 1), (512, 4), (2048, 16))
PARTIAL_ROT_DIM = 16
ROPE_THETA = 500000.0
D_FF = 2816
Q_BLOCK = 128
NORM_EPS = 1e-6
MASK_VALUE = -1e30
HALF_WINDOW = 64
assert all(w // (2 * d) == HALF_WINDOW for w, d in DIL_PAIRS)
LOG2_E = 1.4426950408889634

LANES = 128
V7X_VMEM_LIMIT_BYTES = 56 * 1024 * 1024

FFN_ROWS = 1024
FFN_HIDDEN_CHUNK = 1024
PROJ_ROWS = 1024
MLA_Q_ROWS = 512
MLA_KEY_TILE = 256
PAIR = 2 * HEAD_DIM
N_PAIRS = N_MLA_HEADS // 2
MLA_PAIR_W = 2 * LANES
HALF_LANES = LANES // 2
MLA_ROPE_HALF = MLA_ROPE_DIM // 2
DIL_ROT_HALF = PARTIAL_ROT_DIM // 2
DIL_HEAD_HALF = HEAD_DIM // 2

_OFF_CQ = 0
_OFF_CKV = _OFF_CQ + MLA_Q_LORA
_OFF_KR = _OFF_CKV + MLA_KV_LORA
_OFF_QD = _OFF_KR + LANES
_OFF_KD = _OFF_QD + D_DIL_OUT
_OFF_VD = _OFF_KD + D_DIL_OUT
_PROJ_W = _OFF_VD + D_DIL_OUT


def _rms(x, g):
    ms = jnp.mean(x * x, axis=-1, keepdims=True)
    return x * lax.rsqrt(ms + NORM_EPS) * g


def _rope(x, c, s):
    return x * c + pltpu.roll(x, LANES // 2, 1) * s


def _swiglu_residual(x, g_ref, wg_ref, wu_ref, wd_ref):
    h = _rms(x, g_ref[...]).astype(BF16)
    out = None
    for lo in range(0, D_FF, FFN_HIDDEN_CHUNK):
        hi = min(lo + FFN_HIDDEN_CHUNK, D_FF)
        gate = jnp.dot(h, wg_ref[:, lo:hi], preferred_element_type=F32)
        up = jnp.dot(h, wu_ref[:, lo:hi], preferred_element_type=F32)
        act = (gate * jax.nn.sigmoid(gate) * up).astype(BF16)
        part = jnp.dot(act, wd_ref[lo:hi, :], preferred_element_type=F32)
        out = part if out is None else out + part
    return x + 0.5 * out


def _ffn1_kernel(x_ref, g_ref, wg_ref, wu_ref, wd_ref, o_ref):
    o_ref[...] = _swiglu_residual(x_ref[...], g_ref, wg_ref, wu_ref, wd_ref)


def _ffn2_kernel(x_ref, om_ref, od_ref, gm_ref, gd_ref, wo_ref,
                 g_ref, wg_ref, wu_ref, wd_ref, gf_ref, o_ref):
    nm = _rms(om_ref[...].astype(F32), gm_ref[...]).astype(BF16)
    nd = _rms(od_ref[...].astype(F32), gd_ref[...]).astype(BF16)
    x = (x_ref[...]
         + jnp.dot(nm, wo_ref[:D_MLA_OUT, :], preferred_element_type=F32)
         + jnp.dot(nd, wo_ref[D_MLA_OUT:, :], preferred_element_type=F32))
    o_ref[...] = _rms(_swiglu_residual(x, g_ref, wg_ref, wu_ref, wd_ref), gf_ref[...])


def _resident(shape):
    return pl.BlockSpec(shape, lambda *_: (0,) * len(shape), pipeline_mode=pl.Buffered(1))


def _ffn1(x2d, g, wg, wu, wd):
    n = x2d.shape[0]
    row = pl.BlockSpec((FFN_ROWS, D_MODEL), lambda i: (i, 0))
    return pl.pallas_call(
        _ffn1_kernel,
        grid=(n // FFN_ROWS,),
        in_specs=[row, _resident((1, D_MODEL)), _resident((D_MODEL, D_FF)),
                  _resident((D_MODEL, D_FF)), _resident((D_FF, D_MODEL))],
        out_specs=row,
        out_shape=jax.ShapeDtypeStruct((n, D_MODEL), F32),
        compiler_params=pltpu.CompilerParams(
            dimension_semantics=("parallel",), vmem_limit_bytes=V7X_VMEM_LIMIT_BYTES),
        name="ffn1",
    )(x2d, g, wg, wu, wd)


def _ffn2(x2d, om, od, gm, gd, wo, g, wg, wu, wd, gf):
    n = x2d.shape[0]
    row = pl.BlockSpec((FFN_ROWS, D_MODEL), lambda i: (i, 0))
    half_row = pl.BlockSpec((FFN_ROWS, D_MLA_OUT), lambda i: (i, 0))
    return pl.pallas_call(
        _ffn2_kernel,
        grid=(n // FFN_ROWS,),
        in_specs=[row, half_row, half_row,
                  _resident((1, D_MLA_OUT)), _resident((1, D_DIL_OUT)),
                  _resident((D_MLA_OUT + D_DIL_OUT, D_MODEL)),
                  _resident((1, D_MODEL)), _resident((D_MODEL, D_FF)),
                  _resident((D_MODEL, D_FF)), _resident((D_FF, D_MODEL)),
                  _resident((1, D_MODEL))],
        out_specs=row,
        out_shape=jax.ShapeDtypeStruct((n, D_MODEL), F32),
        compiler_params=pltpu.CompilerParams(
            dimension_semantics=("parallel",), vmem_limit_bytes=V7X_VMEM_LIMIT_BYTES),
        name="ffn2",
    )(x2d, om, od, gm, gd, wo, g, wg, wu, wd, gf)


def _proj_kernel(x_ref, g_ref, win_ref, gq_ref, wuq_ref, gkv_ref, wukv_ref, tab_ref,
                 q_ref, kc_ref, v_ref, qd_ref, kd_ref, vd_ref):
    h = _rms(x_ref[...], g_ref[...]).astype(BF16)
    lat = jnp.dot(h, win_ref[:, :_OFF_QD], preferred_element_type=F32)
    dil = jnp.dot(h, win_ref[:, _OFF_QD:], preferred_element_type=F32)

    cm, sm, cd, sd = (tab_ref[:, i * LANES:(i + 1) * LANES] for i in range(4))
    rope_m = functools.partial(_rope, c=cm, s=sm)
    rope_d = functools.partial(_rope, c=cd, s=sd)

    cq = _rms(lat[:, _OFF_CQ:_OFF_CQ + MLA_Q_LORA], gq_ref[...]).astype(BF16)
    ckv = _rms(lat[:, _OFF_CKV:_OFF_CKV + MLA_KV_LORA], gkv_ref[...]).astype(BF16)
    q = jnp.dot(cq, wuq_ref[...], preferred_element_type=F32)

    d_scale = LOG2_E * HEAD_DIM ** -0.5
    for p in range(N_DIL_HEADS // 2):
        lo = p * PAIR
        qd_ref[:, lo:lo + PAIR] = rope_d(dil[:, lo:lo + PAIR]) * d_scale

    kv = jnp.dot(ckv, wukv_ref[...], preferred_element_type=F32)

    for p in range(N_DIL_HEADS // 2):
        lo = D_DIL_OUT + p * PAIR
        kd_ref[:, p * PAIR:(p + 1) * PAIR] = rope_d(dil[:, lo:lo + PAIR])
    vd_ref[...] = dil[:, 2 * D_DIL_OUT:]

    q_scale = LOG2_E * (MLA_NOPE_DIM + MLA_ROPE_DIM) ** -0.5
    for p in range(N_PAIRS):
        lo = p * MLA_PAIR_W
        q_ref[:, lo:lo + LANES] = (q[:, lo:lo + LANES] * q_scale).astype(BF16)
        q_ref[:, lo + LANES:lo + 2 * LANES] = (
            rope_m(q[:, lo + LANES:lo + 2 * LANES]) * q_scale).astype(BF16)

    k_rope = rope_m(lat[:, _OFF_KR:_OFF_KR + LANES]).astype(BF16)
    for p in range(N_PAIRS):
        lo = p * MLA_PAIR_W
        kc_ref[:, lo:lo + LANES] = kv[:, p * PAIR:(p + 1) * PAIR].astype(BF16)
        kc_ref[:, lo + LANES:lo + 2 * LANES] = k_rope
    v_ref[...] = kv[:, D_MLA_OUT:].astype(BF16)


def _proj(x3d, g, win, gq, wuq, gkv, wukv, tab):
    b, s, _ = x3d.shape

    def tok(width):
        return pl.BlockSpec((None, PROJ_ROWS, width), lambda si, bi: (bi, si, 0))

    def out(width, dtype):
        return jax.ShapeDtypeStruct((b, s, width), dtype)

    return pl.pallas_call(
        _proj_kernel,
        grid=(s // PROJ_ROWS, b),
        in_specs=[tok(D_MODEL), _resident((1, D_MODEL)), _resident((D_MODEL, _PROJ_W)),
                  _resident((1, MLA_Q_LORA)), _resident((MLA_Q_LORA, N_PAIRS * MLA_PAIR_W)),
                  _resident((1, MLA_KV_LORA)), _resident((MLA_KV_LORA, 2 * D_MLA_OUT)),
                  pl.BlockSpec((PROJ_ROWS, 4 * LANES), lambda si, bi: (si, 0))],
        out_specs=[tok(N_PAIRS * MLA_PAIR_W), tok(N_PAIRS * MLA_PAIR_W), tok(D_MLA_OUT),
                   tok(D_DIL_OUT), tok(D_DIL_OUT), tok(D_DIL_OUT)],
        out_shape=[out(N_PAIRS * MLA_PAIR_W, BF16), out(N_PAIRS * MLA_PAIR_W, BF16),
                   out(D_MLA_OUT, BF16), out(D_DIL_OUT, F32), out(D_DIL_OUT, F32),
                   out(D_DIL_OUT, F32)],
        compiler_params=pltpu.CompilerParams(
            dimension_semantics=("parallel", "parallel"),
            vmem_limit_bytes=V7X_VMEM_LIMIT_BYTES),
        name="proj",
    )(x3d, g, win, gq, wuq, gkv, wukv, tab)


def _mla_kernel(q_ref, kc_ref, v_ref, o_ref, s_ref):
    s_len = kc_ref.shape[0]
    n_tiles = s_len // MLA_KEY_TILE
    v = v_ref[...]
    first_v = lax.broadcasted_iota(jnp.int32, v.shape, 1) < MLA_V_DIM
    one = jnp.ones_like(v)
    v_heads = (jnp.where(first_v, v, one), jnp.where(first_v, one, v))
    lane_q = lax.broadcasted_iota(jnp.int32, (MLA_Q_ROWS, MLA_PAIR_W), 1)
    first_o = lax.broadcasted_iota(jnp.int32, (MLA_Q_ROWS, PAIR), 1) < MLA_V_DIM
    chains = [(c, h) for c in range(s_len // MLA_Q_ROWS) for h in range(2)]

    def masked_q(c, h):
        q = q_ref[c * MLA_Q_ROWS:(c + 1) * MLA_Q_ROWS, :]
        nope = (lane_q >= h * MLA_NOPE_DIM) & (lane_q < (h + 1) * MLA_NOPE_DIM)
        in_half = lane_q & (HALF_LANES - 1)
        rope = ((lane_q >= LANES) & (in_half >= h * MLA_ROPE_HALF)
                & (in_half < (h + 1) * MLA_ROPE_HALF))
        return jnp.where(nope | rope, q, jnp.zeros_like(q))

    def score_tile(slot, qh, t, run_max):
        keys = slice(t * MLA_KEY_TILE, (t + 1) * MLA_KEY_TILE)
        s_t = lax.dot_general(qh, kc_ref[keys, :], (((1,), (1,)), ((), ())),
                              preferred_element_type=F32)
        s_ref[slot, :, keys] = s_t
        tile_max = functools.reduce(
            jnp.maximum, [s_t[:, i:i + LANES] for i in range(0, MLA_KEY_TILE, LANES)])
        return tile_max if run_max is None else jnp.maximum(run_max, tile_max)

    def pv_tile(slot, h, t, m, acc):
        keys = slice(t * MLA_KEY_TILE, (t + 1) * MLA_KEY_TILE)
        p_t = jnp.exp2((s_ref[slot, :, keys] - m).astype(BF16))
        d = jnp.dot(p_t, v_heads[h][keys, :], preferred_element_type=F32)
        return d if acc is None else acc + d

    qh = masked_q(*chains[0])
    run_max = None
    for t in range(n_tiles):
        run_max = score_tile(0, qh, t, run_max)
    m = jnp.max(run_max, axis=-1, keepdims=True)
    res = []
    for i, (c, h) in enumerate(chains):
        nxt = chains[i + 1] if i + 1 < len(chains) else None
        if nxt is not None:
            qh = masked_q(*nxt)
        run_max, acc = None, None
        for t in range(n_tiles):
            if nxt is not None:
                run_max = score_tile((i + 1) % 2, qh, t, run_max)
            acc = pv_tile(i % 2, h, t, m, acc)
        if nxt is not None:
            m = jnp.max(run_max, axis=-1, keepdims=True)
        res.append(acc)
        if h == 1:
            num = jnp.where(first_o, res[0], res[1])
            den = pltpu.roll(jnp.where(first_o, res[1], res[0]), MLA_V_DIM, 1)
            o_ref[c * MLA_Q_ROWS:(c + 1) * MLA_Q_ROWS, :] = (num / den).astype(o_ref.dtype)
            res = []


def _mla(q, kc, v):
    b, s, _ = q.shape
    return pl.pallas_call(
        _mla_kernel,
        grid=(b, N_PAIRS),
        in_specs=[pl.BlockSpec((None, s, MLA_PAIR_W), lambda bi, p: (bi, 0, p)),
                  pl.BlockSpec((None, s, MLA_PAIR_W), lambda bi, p: (bi, 0, p)),
                  pl.BlockSpec((None, s, PAIR), lambda bi, p: (bi, 0, p))],
        out_specs=pl.BlockSpec((None, s, PAIR), lambda bi, p: (bi, 0, p)),
        out_shape=jax.ShapeDtypeStruct((b, s, D_MLA_OUT), BF16),
        scratch_shapes=[pltpu.VMEM((2, MLA_Q_ROWS, s), F32)],
        compiler_params=pltpu.CompilerParams(
            dimension_semantics=("parallel", "parallel"),
            vmem_limit_bytes=V7X_VMEM_LIMIT_BYTES),
        name="mla",
    )(q, kc, v)


DIL_SPAN = Q_BLOCK + 2 * HALF_WINDOW
DIL_MID = DIL_PAIRS[1][1]
assert DIL_PAIRS[0][1] == 1 and DIL_PAIRS[2][1] == DIL_MID * DIL_MID


def _band_bias():
    row = np.arange(Q_BLOCK)[:, None]
    col = np.arange(DIL_SPAN)[None, :]
    kinds = [np.abs(col - row - off) <= HALF_WINDOW
             for off in (0, HALF_WINDOW, 2 * HALF_WINDOW)]
    bias = [np.where(k, 0.0, MASK_VALUE) for k in kinds]
    short = bias[0][:, :Q_BLOCK]
    rows = [np.concatenate([b, b], axis=1) for b in bias]
    rows.append(np.concatenate([short, short, np.zeros_like(bias[0])], axis=1))
    return np.stack(rows).astype(np.float32)


def _band_scores(q, k_heads, bias):
    return lax.dot_general(q, k_heads, (((1,), (1,)), ((), ())),
                           preferred_element_type=F32) + bias


def _band_max(s):
    span = s.shape[1] // 2
    return (jnp.max(s[:, :span], axis=-1, keepdims=True),
            jnp.max(s[:, span:], axis=-1, keepdims=True))


def _band_probs(s, m):
    span = s.shape[1] // 2
    return jnp.concatenate([jnp.exp2((s[:, :span] - m[0]).astype(BF16)),
                            jnp.exp2((s[:, span:] - m[1]).astype(BF16))], axis=1)


def _band_finish(p, m, v_heads):
    res = jnp.dot(p, v_heads, preferred_element_type=F32)
    first = lax.broadcasted_iota(jnp.int32, (Q_BLOCK, PAIR), 1) < HEAD_DIM
    m_lanes = jnp.where(first, jnp.broadcast_to(m[0], (Q_BLOCK, PAIR)),
                        jnp.broadcast_to(m[1], (Q_BLOCK, PAIR)))
    return res[:, :PAIR], res[:, PAIR:], m_lanes


def _dil_kernel(q_ref, k_ref, v_ref, bias_ref, o_ref, mid_in_ref, nat_ref, mid_ref):
    s_len = q_ref.shape[0]
    mid_len = s_len // DIL_MID
    far_len = mid_len // DIL_MID
    lane = lax.broadcasted_iota(jnp.int32, (1, PAIR), 1)
    first = (lane & (HALF_LANES - 1)) < DIL_HEAD_HALF
    first_v = lane < HEAD_DIM

    for t, ref in enumerate((q_ref, k_ref, v_ref)):
        for r in range(DIL_MID):
            mid_in_ref[t, r] = ref[pl.ds(r, mid_len, stride=DIL_MID), :]

    groups = []

    def write_near(j, vals):
        for t, val in enumerate(vals):
            nat_ref[t, j * Q_BLOCK:(j + 1) * Q_BLOCK, :] = val
    groups.append((lambda: (q_ref[...], k_ref[...], v_ref[...]), s_len, write_near))

    for r in range(DIL_MID):
        def write_mid(j, vals, r=r):
            for t, val in enumerate(vals):
                mid_ref[0, t, r, j * Q_BLOCK:(j + 1) * Q_BLOCK, :] = val
        groups.append((lambda r=r: tuple(mid_in_ref[t, r] for t in range(3)), mid_len, write_mid))

    for r in range(DIL_MID):
        for rr in range(DIL_MID):
            sl = pl.ds(rr, far_len, stride=DIL_MID)
            def write_far(j, vals, r=r, sl=sl):
                for t, val in enumerate(vals):
                    mid_ref[1, t, r, sl, :] = val
            groups.append((lambda r=r, sl=sl: tuple(mid_in_ref[t, r, sl, :] for t in range(3)),
                           far_len, write_far))

    blocks = [(g, j) for g, (_, sub_len, _) in enumerate(groups)
              for j in range(sub_len // Q_BLOCK)]
    operands = {}

    def scores(g, j):
        load, sub_len, _ = groups[g]
        if g not in operands:
            q, k, v = load()
            one_a = jnp.broadcast_to(first_v.astype(F32), v.shape)
            operands.clear()
            operands[g] = (
                q.astype(BF16),
                jnp.where(first, k, 0.0).astype(BF16), jnp.where(first, 0.0, k).astype(BF16),
                jnp.concatenate([jnp.where(first_v, v, 0.0), one_a], axis=1).astype(BF16),
                jnp.concatenate([jnp.where(first_v, 0.0, v), 1.0 - one_a], axis=1).astype(BF16))
        qb, ka, kb, va, vb = operands[g]
        span = min(DIL_SPAN, sub_len)
        q0 = j * Q_BLOCK
        k0 = min(max(q0 - HALF_WINDOW, 0), sub_len - span)
        kind = 3 if span < DIL_SPAN else (q0 - k0) // HALF_WINDOW
        keys = slice(k0, k0 + span)
        k_heads = jnp.concatenate([ka[keys], kb[keys]], axis=0)
        v_heads = jnp.concatenate([va[keys], vb[keys]], axis=0)
        return (_band_scores(qb[q0:q0 + Q_BLOCK], k_heads, bias_ref[kind][:, :2 * span]),
                v_heads)

    n = len(blocks)
    sc = {i: scores(*blocks[i]) for i in range(min(3, n))}
    mx = {i: _band_max(sc[i][0]) for i in range(min(2, n))}
    pr = {0: _band_probs(sc[0][0], mx[0])}
    for i, (g, j) in enumerate(blocks):
        if i + 3 < n:
            sc[i + 3] = scores(*blocks[i + 3])
        if i + 2 < n:
            mx[i + 2] = _band_max(sc[i + 2][0])
        if i + 1 < n:
            pr[i + 1] = _band_probs(sc[i + 1][0], mx[i + 1])
        groups[g][2](j, _band_finish(pr.pop(i), mx.pop(i), sc.pop(i)[1]))

    for r in range(DIL_MID):
        nat = pl.ds(r, mid_len, stride=DIL_MID)
        accs = (nat_ref[0, nat, :], mid_ref[0, 0, r], mid_ref[1, 0, r])
        ls = (nat_ref[1, nat, :], mid_ref[0, 1, r], mid_ref[1, 1, r])
        ms = (nat_ref[2, nat, :], mid_ref[0, 2, r], mid_ref[1, 2, r])
        top = jnp.maximum(jnp.maximum(ms[0], ms[1]), ms[2])
        es = [jnp.exp2(m - top) for m in ms]
        num = es[0] * accs[0] + es[1] * accs[1] + es[2] * accs[2]
        den = es[0] * ls[0] + es[1] * ls[1] + es[2] * ls[2]
        o_ref[nat, :] = num / den


def _dilated(qd, kd, vd):
    b, s, _ = qd.shape
    blk = pl.BlockSpec((None, s, PAIR), lambda bi, p: (bi, 0, p))
    bias = jnp.asarray(_band_bias())
    return pl.pallas_call(
        _dil_kernel,
        grid=(b, N_DIL_HEADS // 2),
        in_specs=[blk, blk, blk, _resident(bias.shape)],
        out_specs=blk,
        out_shape=jax.ShapeDtypeStruct((b, s, D_DIL_OUT), F32),
        scratch_shapes=[pltpu.VMEM((3, DIL_MID, s // DIL_MID, PAIR), F32),
                        pltpu.VMEM((3, s, PAIR), F32),
                        pltpu.VMEM((2, 3, DIL_MID, s // DIL_MID, PAIR), F32)],
        compiler_params=pltpu.CompilerParams(
            dimension_semantics=("parallel", "parallel"),
            vmem_limit_bytes=V7X_VMEM_LIMIT_BYTES),
        name="dilated",
    )(qd, kd, vd, bias)


def _rope_lane_tables(seq):
    lane = np.arange(LANES)
    first = lane < HALF_LANES

    def tables(rot_dim, rotary_lanes, period):
        half = rot_dim // 2
        u = lane % period
        rot = u < rotary_lanes
        inv_freq = ROPE_THETA ** (-jnp.arange(half, dtype=F32) * (2.0 / rot_dim))
        ang = jnp.arange(seq, dtype=F32)[:, None] * inv_freq[u % half][None, :]
        cos, sin = jnp.cos(ang), jnp.sin(ang)
        return [jnp.where(rot, cos, 1.0),
                jnp.where(rot & first, -sin, jnp.where(rot & ~first, sin, 0.0))]

    return jnp.concatenate(tables(MLA_ROPE_DIM, 2 * MLA_ROPE_HALF, HALF_LANES)
                           + tables(PARTIAL_ROT_DIM, DIL_ROT_HALF, DIL_HEAD_HALF), axis=-1)


def _reorder_w_in(w_in):
    c_q = w_in[:, :MLA_Q_LORA]
    c_kv = w_in[:, MLA_Q_LORA:MLA_Q_LORA + MLA_KV_LORA]
    k_r = w_in[:, MLA_Q_LORA + MLA_KV_LORA:MLA_Q_LORA + MLA_KV_LORA + MLA_ROPE_DIM]
    rest = w_in[:, MLA_Q_LORA + MLA_KV_LORA + MLA_ROPE_DIM:]
    d_in = w_in.shape[0]
    pad = np.zeros((d_in, HALF_LANES - 2 * MLA_ROPE_HALF), w_in.dtype)
    k_x1, k_x2 = k_r[:, :MLA_ROPE_HALF], k_r[:, MLA_ROPE_HALF:]
    k_block = np.concatenate([k_x1, k_x1, pad, k_x2, k_x2, pad], axis=-1)
    q_d, k_d, v_d = (rest[:, i * D_DIL_OUT:(i + 1) * D_DIL_OUT] for i in range(3))
    return np.concatenate([c_q, c_kv, k_block, _dil_rotary_layout(q_d),
                           _dil_rotary_layout(k_d), v_d], axis=-1)


def _dil_rotary_layout(w):
    w = w.reshape(w.shape[0], N_DIL_HEADS // 2, 2, HEAD_DIM)
    n_pass = DIL_HEAD_HALF - DIL_ROT_HALF
    x1 = w[..., :DIL_ROT_HALF]
    x2 = w[..., DIL_ROT_HALF:PARTIAL_ROT_DIM]
    p1 = w[..., PARTIAL_ROT_DIM:PARTIAL_ROT_DIM + n_pass]
    p2 = w[..., PARTIAL_ROT_DIM + n_pass:]
    halves = [np.concatenate([x[:, :, 0], p[:, :, 0], x[:, :, 1], p[:, :, 1]], axis=-1)
              for x, p in ((x1, p1), (x2, p2))]
    return np.concatenate(halves, axis=-1).reshape(w.shape[0], D_DIL_OUT)


def _reorder_w_uq(w_uq):
    rows = w_uq.shape[0]
    w = w_uq.reshape(rows, N_PAIRS, 2, MLA_NOPE_DIM + MLA_ROPE_DIM)
    nope = w[..., :MLA_NOPE_DIM].reshape(rows, N_PAIRS, 2 * MLA_NOPE_DIM)
    rope = w[..., MLA_NOPE_DIM:]
    pad = np.zeros((rows, N_PAIRS, HALF_LANES - 2 * MLA_ROPE_HALF), w_uq.dtype)
    x1, x2 = rope[..., :MLA_ROPE_HALF], rope[..., MLA_ROPE_HALF:]
    block = [x1[:, :, 0], x1[:, :, 1], pad, x2[:, :, 0], x2[:, :, 1], pad]
    return np.concatenate([nope] + block, axis=-1).reshape(rows, N_PAIRS * MLA_PAIR_W)


def _reorder_w_ukv(w_ukv):
    rows = w_ukv.shape[0]
    w = w_ukv.reshape(rows, N_MLA_HEADS, MLA_NOPE_DIM + MLA_V_DIM)
    k_nope = w[..., :MLA_NOPE_DIM].reshape(rows, D_MLA_OUT)
    v = w[..., MLA_NOPE_DIM:].reshape(rows, D_MLA_OUT)
    return np.concatenate([k_nope, v], axis=-1)


def _column_source(layout, n_cols):
    idx = layout(np.arange(1, n_cols + 1)[None, :])[0]
    return np.where(idx == 0, n_cols, idx - 1).astype(np.int32)


def _permute_columns(w, src):
    w = w.astype(BF16)
    onehot = (lax.broadcasted_iota(jnp.int32, (w.shape[1], src.shape[0]), 0)
              == jnp.asarray(src)[None, :])
    return jnp.dot(w, onehot.astype(BF16), preferred_element_type=F32).astype(BF16)


def kernel(x, ffn1_norm, ffn1_w_gate, ffn1_w_up, ffn1_w_down, mix_norm, w_in, mla_q_norm, mla_w_uq, mla_kv_norm, mla_w_ukv, mla_out_norm, dil_out_norm, w_out, ffn2_norm, ffn2_w_gate, ffn2_w_up, ffn2_w_down, final_norm):
    b, s, d = x.shape
    assert ffn1_norm.shape[0] == 1, "single-layer block: the final norm is fused into its last FFN"
    assert d == D_MODEL and s % MLA_Q_ROWS == 0 and s % (Q_BLOCK * DIL_PAIRS[-1][1]) == 0
    tab = _rope_lane_tables(s)
    x1 = _ffn1(x.reshape(b * s, d), ffn1_norm[0][None], ffn1_w_gate[0].astype(BF16),
               ffn1_w_up[0].astype(BF16), ffn1_w_down[0].astype(BF16))
    q, kc, v, qd, kd, vd = _proj(
        x1.reshape(b, s, d), mix_norm[0][None],
        _permute_columns(w_in[0], _column_source(_reorder_w_in, w_in.shape[2])),
        mla_q_norm[0][None],
        _permute_columns(mla_w_uq[0], _column_source(_reorder_w_uq, mla_w_uq.shape[2])),
        mla_kv_norm[0][None],
        _permute_columns(mla_w_ukv[0], _column_source(_reorder_w_ukv, mla_w_ukv.shape[2])), tab)
    o_mla = _mla(q, kc, v).reshape(b * s, D_MLA_OUT)
    o_dil = _dilated(qd, kd, vd).reshape(b * s, D_DIL_OUT)
    y = _ffn2(x1, o_mla, o_dil, mla_out_norm[0][None], dil_out_norm[0][None],
              w_out[0].astype(BF16), ffn2_norm[0][None],
              ffn2_w_gate[0].astype(BF16), ffn2_w_up[0].astype(BF16),
              ffn2_w_down[0].astype(BF16), final_norm[None])
    return y.reshape(b, s, d)
```

```python
import functools

import numpy as np
import jax
import jax.numpy as jnp
from jax import lax
from jax.experimental import pallas as pl
from jax.experimental.pallas import tpu as pltpu

F32 = jnp.float32
BF16 = jnp.bfloat16

D_MODEL = 1024
HEAD_DIM = 64
N_MLA_HEADS = 8
N_DIL_HEADS = 8
MLA_NOPE_DIM = 64
MLA_ROPE_DIM = 32
MLA_V_DIM = 64
MLA_Q_LORA = 384
MLA_KV_LORA = 256
D_MLA_OUT = N_MLA_HEADS * MLA_V_DIM
D_DIL_OUT = N_DIL_HEADS * HEAD_DIM
DIL_PAIRS = ((128, 1), (512, 4), (2048, 16))
PARTIAL_ROT_DIM = 16
ROPE_THETA = 500000.0
D_FF = 2816
Q_BLOCK = 128
NORM_EPS = 1e-6
MASK_VALUE = -1e30
HALF_WINDOW = 64
assert all(w // (2 * d) == HALF_WINDOW for w, d in DIL_PAIRS)
LOG2_E = 1.4426950408889634

LANES = 128
V7X_VMEM_LIMIT_BYTES = 56 * 1024 * 1024

FFN_ROWS = 1024
FFN_HIDDEN_CHUNK = 1024
PROJ_ROWS = 1024
MLA_Q_ROWS = 512
MLA_KEY_TILE = 256
MLA_PAIRS_PER_STEP = 2
PAIR = 2 * HEAD_DIM
N_PAIRS = N_MLA_HEADS // 2
MLA_PAIR_W = 2 * LANES
HALF_LANES = LANES // 2
MLA_ROPE_HALF = MLA_ROPE_DIM // 2
DIL_ROT_HALF = PARTIAL_ROT_DIM // 2
DIL_HEAD_HALF = HEAD_DIM // 2

_OFF_CQ = 0
_OFF_CKV = _OFF_CQ + MLA_Q_LORA
_OFF_KR = _OFF_CKV + MLA_KV_LORA
_OFF_QD = _OFF_KR + LANES
_OFF_KD = _OFF_QD + D_DIL_OUT
_OFF_VD = _OFF_KD + D_DIL_OUT
_PROJ_W = _OFF_VD + D_DIL_OUT


def _rms(x, g):
    ms = jnp.mean(x * x, axis=-1, keepdims=True)
    return x * lax.rsqrt(ms + NORM_EPS) * g


def _rope(x, c, s):
    return x * c + pltpu.roll(x, LANES // 2, 1) * s


def _swiglu_residual(x, g_ref, wg_ref, wu_ref, wd_ref):
    h = _rms(x, g_ref[...]).astype(BF16)
    out = None
    for lo in range(0, D_FF, FFN_HIDDEN_CHUNK):
        hi = min(lo + FFN_HIDDEN_CHUNK, D_FF)
        gate = jnp.dot(h, wg_ref[:, lo:hi], preferred_element_type=F32)
        up = jnp.dot(h, wu_ref[:, lo:hi], preferred_element_type=F32)
        act = (gate * jax.nn.sigmoid(gate) * up).astype(BF16)
        part = jnp.dot(act, wd_ref[lo:hi, :], preferred_element_type=F32)
        out = part if out is None else out + part
    return x + 0.5 * out


def _ffn1_kernel(x_ref, g_ref, wg_ref, wu_ref, wd_ref, o_ref):
    o_ref[...] = _swiglu_residual(x_ref[...], g_ref, wg_ref, wu_ref, wd_ref)


def _ffn2_kernel(x_ref, om_ref, od_ref, gm_ref, gd_ref, wo_ref,
                 g_ref, wg_ref, wu_ref, wd_ref, gf_ref, o_ref):
    nm = _rms(om_ref[...].astype(F32), gm_ref[...]).astype(BF16)
    nd = _rms(od_ref[...].astype(F32), gd_ref[...]).astype(BF16)
    x = (x_ref[...]
         + jnp.dot(nm, wo_ref[:D_MLA_OUT, :], preferred_element_type=F32)
         + jnp.dot(nd, wo_ref[D_MLA_OUT:, :], preferred_element_type=F32))
    o_ref[...] = _rms(_swiglu_residual(x, g_ref, wg_ref, wu_ref, wd_ref), gf_ref[...])


def _resident(shape):
    return pl.BlockSpec(shape, lambda *_: (0,) * len(shape), pipeline_mode=pl.Buffered(1))


def _ffn1(x2d, g, wg, wu, wd):
    n = x2d.shape[0]
    row = pl.BlockSpec((FFN_ROWS, D_MODEL), lambda i: (i, 0))
    return pl.pallas_call(
        _ffn1_kernel,
        grid=(n // FFN_ROWS,),
        in_specs=[row, _resident((1, D_MODEL)), _resident((D_MODEL, D_FF)),
                  _resident((D_MODEL, D_FF)), _resident((D_FF, D_MODEL))],
        out_specs=row,
        out_shape=jax.ShapeDtypeStruct((n, D_MODEL), F32),
        compiler_params=pltpu.CompilerParams(
            dimension_semantics=("parallel",), vmem_limit_bytes=V7X_VMEM_LIMIT_BYTES),
        name="ffn1",
    )(x2d, g, wg, wu, wd)


def _ffn2(x2d, om, od, gm, gd, wo, g, wg, wu, wd, gf):
    n = x2d.shape[0]
    row = pl.BlockSpec((FFN_ROWS, D_MODEL), lambda i: (i, 0))
    half_row = pl.BlockSpec((FFN_ROWS, D_MLA_OUT), lambda i: (i, 0))
    return pl.pallas_call(
        _ffn2_kernel,
        grid=(n // FFN_ROWS,),
        in_specs=[row, half_row, half_row,
                  _resident((1, D_MLA_OUT)), _resident((1, D_DIL_OUT)),
                  _resident((D_MLA_OUT + D_DIL_OUT, D_MODEL)),
                  _resident((1, D_MODEL)), _resident((D_MODEL, D_FF)),
                  _resident((D_MODEL, D_FF)), _resident((D_FF, D_MODEL)),
                  _resident((1, D_MODEL))],
        out_specs=row,
        out_shape=jax.ShapeDtypeStruct((n, D_MODEL), F32),
        compiler_params=pltpu.CompilerParams(
            dimension_semantics=("parallel",), vmem_limit_bytes=V7X_VMEM_LIMIT_BYTES),
        name="ffn2",
    )(x2d, om, od, gm, gd, wo, g, wg, wu, wd, gf)


def _proj_kernel(x_ref, g_ref, win_ref, gq_ref, wuq_ref, gkv_ref, wukv_ref, tab_ref,
                 q_ref, kc_ref, v_ref, qd_ref, kd_ref, vd_ref):
    h = _rms(x_ref[...], g_ref[...]).astype(BF16)
    lat = jnp.dot(h, win_ref[:, :_OFF_QD], preferred_element_type=F32)
    dil = jnp.dot(h, win_ref[:, _OFF_QD:], preferred_element_type=F32)

    cm, sm, cd, sd = (tab_ref[:, i * LANES:(i + 1) * LANES] for i in range(4))
    rope_m = functools.partial(_rope, c=cm, s=sm)
    rope_d = functools.partial(_rope, c=cd, s=sd)

    cq = _rms(lat[:, _OFF_CQ:_OFF_CQ + MLA_Q_LORA], gq_ref[...]).astype(BF16)
    ckv = _rms(lat[:, _OFF_CKV:_OFF_CKV + MLA_KV_LORA], gkv_ref[...]).astype(BF16)
    q = jnp.dot(cq, wuq_ref[...], preferred_element_type=F32)

    d_scale = LOG2_E * HEAD_DIM ** -0.5
    for p in range(N_DIL_HEADS // 2):
        lo = p * PAIR
        qd_ref[:, lo:lo + PAIR] = rope_d(dil[:, lo:lo + PAIR]) * d_scale

    kv = jnp.dot(ckv, wukv_ref[...], preferred_element_type=F32)

    for p in range(N_DIL_HEADS // 2):
        lo = D_DIL_OUT + p * PAIR
        kd_ref[:, p * PAIR:(p + 1) * PAIR] = rope_d(dil[:, lo:lo + PAIR])
    vd_ref[...] = dil[:, 2 * D_DIL_OUT:]

    q_scale = LOG2_E * (MLA_NOPE_DIM + MLA_ROPE_DIM) ** -0.5
    for p in range(N_PAIRS):
        lo = p * MLA_PAIR_W
        q_ref[:, lo:lo + LANES] = (q[:, lo:lo + LANES] * q_scale).astype(BF16)
        q_ref[:, lo + LANES:lo + 2 * LANES] = (
            rope_m(q[:, lo + LANES:lo + 2 * LANES]) * q_scale).astype(BF16)

    k_rope = rope_m(lat[:, _OFF_KR:_OFF_KR + LANES]).astype(BF16)
    for p in range(N_PAIRS):
        lo = p * MLA_PAIR_W
        kc_ref[:, lo:lo + LANES] = kv[:, p * PAIR:(p + 1) * PAIR].astype(BF16)
        kc_ref[:, lo + LANES:lo + 2 * LANES] = k_rope
    v_ref[...] = kv[:, D_MLA_OUT:].astype(BF16)


def _proj(x3d, g, win, gq, wuq, gkv, wukv, tab):
    b, s, _ = x3d.shape

    def tok(width):
        return pl.BlockSpec((None, PROJ_ROWS, width), lambda si, bi: (bi, si, 0))

    def out(width, dtype):
        return jax.ShapeDtypeStruct((b, s, width), dtype)

    return pl.pallas_call(
        _proj_kernel,
        grid=(s // PROJ_ROWS, b),
        in_specs=[tok(D_MODEL), _resident((1, D_MODEL)), _resident((D_MODEL, _PROJ_W)),
                  _resident((1, MLA_Q_LORA)), _resident((MLA_Q_LORA, N_PAIRS * MLA_PAIR_W)),
                  _resident((1, MLA_KV_LORA)), _resident((MLA_KV_LORA, 2 * D_MLA_OUT)),
                  pl.BlockSpec((PROJ_ROWS, 4 * LANES), lambda si, bi: (si, 0))],
        out_specs=[tok(N_PAIRS * MLA_PAIR_W), tok(N_PAIRS * MLA_PAIR_W), tok(D_MLA_OUT),
                   tok(D_DIL_OUT), tok(D_DIL_OUT), tok(D_DIL_OUT)],
        out_shape=[out(N_PAIRS * MLA_PAIR_W, BF16), out(N_PAIRS * MLA_PAIR_W, BF16),
                   out(D_MLA_OUT, BF16), out(D_DIL_OUT, F32), out(D_DIL_OUT, F32),
                   out(D_DIL_OUT, F32)],
        compiler_params=pltpu.CompilerParams(
            dimension_semantics=("parallel", "parallel"),
            vmem_limit_bytes=V7X_VMEM_LIMIT_BYTES),
        name="proj",
    )(x3d, g, win, gq, wuq, gkv, wukv, tab)


def _mla_kernel(q_ref, kc_ref, v_ref, o_ref, s_ref):
    s_len = kc_ref.shape[0]
    n_tiles = s_len // MLA_KEY_TILE
    lane_q = lax.broadcasted_iota(jnp.int32, (MLA_Q_ROWS, MLA_PAIR_W), 1)
    first_o = lax.broadcasted_iota(jnp.int32, (MLA_Q_ROWS, PAIR), 1) < MLA_V_DIM
    chains = [(p, c, h) for p in range(MLA_PAIRS_PER_STEP)
              for c in range(s_len // MLA_Q_ROWS) for h in range(2)]
    pair_values = {}

    def v_heads(p):
        if p not in pair_values:
            v = v_ref[:, p * PAIR:(p + 1) * PAIR]
            first_v = lax.broadcasted_iota(jnp.int32, v.shape, 1) < MLA_V_DIM
            one = jnp.ones_like(v)
            pair_values.clear()
            pair_values[p] = (jnp.where(first_v, v, one), jnp.where(first_v, one, v))
        return pair_values[p]

    def masked_q(p, c, h):
        q = q_ref[c * MLA_Q_ROWS:(c + 1) * MLA_Q_ROWS, p * MLA_PAIR_W:(p + 1) * MLA_PAIR_W]
        nope = (lane_q >= h * MLA_NOPE_DIM) & (lane_q < (h + 1) * MLA_NOPE_DIM)
        in_half = lane_q & (HALF_LANES - 1)
        rope = ((lane_q >= LANES) & (in_half >= h * MLA_ROPE_HALF)
                & (in_half < (h + 1) * MLA_ROPE_HALF))
        return jnp.where(nope | rope, q, jnp.zeros_like(q))

    def score_tile(slot, p, qh, t, run_max):
        keys = slice(t * MLA_KEY_TILE, (t + 1) * MLA_KEY_TILE)
        s_t = lax.dot_general(qh, kc_ref[keys, p * MLA_PAIR_W:(p + 1) * MLA_PAIR_W],
                              (((1,), (1,)), ((), ())), preferred_element_type=F32)
        s_ref[slot, :, keys] = s_t
        tile_max = functools.reduce(
            jnp.maximum, [s_t[:, i:i + LANES] for i in range(0, MLA_KEY_TILE, LANES)])
        return tile_max if run_max is None else jnp.maximum(run_max, tile_max)

    def pv_tile(slot, p, h, t, m, acc):
        keys = slice(t * MLA_KEY_TILE, (t + 1) * MLA_KEY_TILE)
        p_t = jnp.exp2((s_ref[slot, :, keys] - m).astype(BF16))
        d = jnp.dot(p_t, v_heads(p)[h][keys, :], preferred_element_type=F32)
        return d if acc is None else acc + d

    qh = masked_q(*chains[0])
    run_max = None
    for t in range(n_tiles):
        run_max = score_tile(0, chains[0][0], qh, t, run_max)
    m = jnp.max(run_max, axis=-1, keepdims=True)
    res = []
    for i, (p, c, h) in enumerate(chains):
        nxt = chains[i + 1] if i + 1 < len(chains) else None
        if nxt is not None:
            qh = masked_q(*nxt)
        run_max, acc = None, None
        for t in range(n_tiles):
            if nxt is not None:
                run_max = score_tile((i + 1) % 2, nxt[0], qh, t, run_max)
            acc = pv_tile(i % 2, p, h, t, m, acc)
        if nxt is not None:
            m = jnp.max(run_max, axis=-1, keepdims=True)
        res.append(acc)
        if h == 1:
            num = jnp.where(first_o, res[0], res[1])
            den = pltpu.roll(jnp.where(first_o, res[1], res[0]), MLA_V_DIM, 1)
            o_ref[c * MLA_Q_ROWS:(c + 1) * MLA_Q_ROWS, p * PAIR:(p + 1) * PAIR] = (
                num / den).astype(o_ref.dtype)
            res = []


def _mla(q, kc, v):
    b, s, _ = q.shape
    n = MLA_PAIRS_PER_STEP
    return pl.pallas_call(
        _mla_kernel,
        grid=(b, N_PAIRS // MLA_PAIRS_PER_STEP),
        in_specs=[pl.BlockSpec((None, s, n * MLA_PAIR_W), lambda bi, p: (bi, 0, p)),
                  pl.BlockSpec((None, s, n * MLA_PAIR_W), lambda bi, p: (bi, 0, p)),
                  pl.BlockSpec((None, s, n * PAIR), lambda bi, p: (bi, 0, p))],
        out_specs=pl.BlockSpec((None, s, n * PAIR), lambda bi, p: (bi, 0, p)),
        out_shape=jax.ShapeDtypeStruct((b, s, D_MLA_OUT), BF16),
        scratch_shapes=[pltpu.VMEM((2, MLA_Q_ROWS, s), F32)],
        compiler_params=pltpu.CompilerParams(
            dimension_semantics=("parallel", "parallel"),
            vmem_limit_bytes=V7X_VMEM_LIMIT_BYTES),
        name="mla",
    )(q, kc, v)


DIL_SPAN = Q_BLOCK + 2 * HALF_WINDOW
DIL_MID = DIL_PAIRS[1][1]
assert DIL_PAIRS[0][1] == 1 and DIL_PAIRS[2][1] == DIL_MID * DIL_MID


def _band_bias():
    row = np.arange(Q_BLOCK)[:, None]
    col = np.arange(DIL_SPAN)[None, :]
    kinds = [np.abs(col - row - off) <= HALF_WINDOW
             for off in (0, HALF_WINDOW, 2 * HALF_WINDOW)]
    bias = [np.where(k, 0.0, MASK_VALUE) for k in kinds]
    short = bias[0][:, :Q_BLOCK]
    rows = [np.concatenate([b, b], axis=1) for b in bias]
    rows.append(np.concatenate([short, short, np.zeros_like(bias[0])], axis=1))
    return np.stack(rows).astype(np.float32)


def _band_scores(q, k_heads, bias):
    return lax.dot_general(q, k_heads, (((1,), (1,)), ((), ())),
                           preferred_element_type=F32) + bias


def _band_max(s):
    span = s.shape[1] // 2
    return (jnp.max(s[:, :span], axis=-1, keepdims=True),
            jnp.max(s[:, span:], axis=-1, keepdims=True))


def _band_probs(s, m):
    span = s.shape[1] // 2
    return jnp.concatenate([jnp.exp2((s[:, :span] - m[0]).astype(BF16)),
                            jnp.exp2((s[:, span:] - m[1]).astype(BF16))], axis=1)


def _band_finish(p, m, v_heads):
    res = jnp.dot(p, v_heads, preferred_element_type=F32)
    first = lax.broadcasted_iota(jnp.int32, (Q_BLOCK, PAIR), 1) < HEAD_DIM
    m_lanes = jnp.where(first, jnp.broadcast_to(m[0], (Q_BLOCK, PAIR)),
                        jnp.broadcast_to(m[1], (Q_BLOCK, PAIR)))
    return res[:, :PAIR], res[:, PAIR:], m_lanes


def _dil_kernel(q_ref, k_ref, v_ref, bias_ref, o_ref, mid_in_ref, nat_ref, mid_ref):
    s_len = q_ref.shape[0]
    mid_len = s_len // DIL_MID
    far_len = mid_len // DIL_MID
    lane = lax.broadcasted_iota(jnp.int32, (1, PAIR), 1)
    first = (lane & (HALF_LANES - 1)) < DIL_HEAD_HALF
    first_v = lane < HEAD_DIM

    for t, ref in enumerate((q_ref, k_ref, v_ref)):
        for r in range(DIL_MID):
            mid_in_ref[t, r] = ref[pl.ds(r, mid_len, stride=DIL_MID), :]

    groups = []

    def write_near(j, vals):
        for t, val in enumerate(vals):
            nat_ref[t, j * Q_BLOCK:(j + 1) * Q_BLOCK, :] = val
    groups.append((lambda: (q_ref[...], k_ref[...], v_ref[...]), s_len, write_near))

    for r in range(DIL_MID):
        def write_mid(j, vals, r=r):
            for t, val in enumerate(vals):
                mid_ref[0, t, r, j * Q_BLOCK:(j + 1) * Q_BLOCK, :] = val
        groups.append((lambda r=r: tuple(mid_in_ref[t, r] for t in range(3)), mid_len, write_mid))

    for r in range(DIL_MID):
        for rr in range(DIL_MID):
            sl = pl.ds(rr, far_len, stride=DIL_MID)
            def write_far(j, vals, r=r, sl=sl):
                for t, val in enumerate(vals):
                    mid_ref[1, t, r, sl, :] = val
            groups.append((lambda r=r, sl=sl: tuple(mid_in_ref[t, r, sl, :] for t in range(3)),
                           far_len, write_far))

    blocks = [(g, j) for g, (_, sub_len, _) in enumerate(groups)
              for j in range(sub_len // Q_BLOCK)]
    operands = {}

    def scores(g, j):
        load, sub_len, _ = groups[g]
        if g not in operands:
            q, k, v = load()
            one_a = jnp.broadcast_to(first_v.astype(F32), v.shape)
            operands.clear()
            operands[g] = (
                q.astype(BF16),
                jnp.where(first, k, 0.0).astype(BF16), jnp.where(first, 0.0, k).astype(BF16),
                jnp.concatenate([jnp.where(first_v, v, 0.0), one_a], axis=1).astype(BF16),
                jnp.concatenate([jnp.where(first_v, 0.0, v), 1.0 - one_a], axis=1).astype(BF16))
        qb, ka, kb, va, vb = operands[g]
        span = min(DIL_SPAN, sub_len)
        q0 = j * Q_BLOCK
        k0 = min(max(q0 - HALF_WINDOW, 0), sub_len - span)
        kind = 3 if span < DIL_SPAN else (q0 - k0) // HALF_WINDOW
        keys = slice(k0, k0 + span)
        k_heads = jnp.concatenate([ka[keys], kb[keys]], axis=0)
        v_heads = jnp.concatenate([va[keys], vb[keys]], axis=0)
        return (_band_scores(qb[q0:q0 + Q_BLOCK], k_heads, bias_ref[kind][:, :2 * span]),
                v_heads)

    n = len(blocks)
    sc = {i: scores(*blocks[i]) for i in range(min(3, n))}
    mx = {i: _band_max(sc[i][0]) for i in range(min(2, n))}
    pr = {0: _band_probs(sc[0][0], mx[0])}
    for i, (g, j) in enumerate(blocks):
        if i + 3 < n:
            sc[i + 3] = scores(*blocks[i + 3])
        if i + 2 < n:
            mx[i + 2] = _band_max(sc[i + 2][0])
        if i + 1 < n:
            pr[i + 1] = _band_probs(sc[i + 1][0], mx[i + 1])
        groups[g][2](j, _band_finish(pr.pop(i), mx.pop(i), sc.pop(i)[1]))

    for r in range(DIL_MID):
        nat = pl.ds(r, mid_len, stride=DIL_MID)
        accs = (nat_ref[0, nat, :], mid_ref[0, 0, r], mid_ref[1, 0, r])
        ls = (nat_ref[1, nat, :], mid_ref[0, 1, r], mid_ref[1, 1, r])
        ms = (nat_ref[2, nat, :], mid_ref[0, 2, r], mid_ref[1, 2, r])
        top = jnp.maximum(jnp.maximum(ms[0], ms[1]), ms[2])
        es = [jnp.exp2(m - top) for m in ms]
        num = es[0] * accs[0] + es[1] * accs[1] + es[2] * accs[2]
        den = es[0] * ls[0] + es[1] * ls[1] + es[2] * ls[2]
        o_ref[nat, :] = num / den


def _dilated(qd, kd, vd):
    b, s, _ = qd.shape
    blk = pl.BlockSpec((None, s, PAIR), lambda bi, p: (bi, 0, p))
    bias = jnp.asarray(_band_bias())
    return pl.pallas_call(
        _dil_kernel,
        grid=(b, N_DIL_HEADS // 2),
        in_specs=[blk, blk, blk, _resident(bias.shape)],
        out_specs=blk,
        out_shape=jax.ShapeDtypeStruct((b, s, D_DIL_OUT), F32),
        scratch_shapes=[pltpu.VMEM((3, DIL_MID, s // DIL_MID, PAIR), F32),
                        pltpu.VMEM((3, s, PAIR), F32),
                        pltpu.VMEM((2, 3, DIL_MID, s // DIL_MID, PAIR), F32)],
        compiler_params=pltpu.CompilerParams(
            dimension_semantics=("parallel", "parallel"),
            vmem_limit_bytes=V7X_VMEM_LIMIT_BYTES),
        name="dilated",
    )(qd, kd, vd, bias)


def _rope_lane_tables(seq):
    lane = np.arange(LANES)
    first = lane < HALF_LANES

    def tables(rot_dim, rotary_lanes, period):
        half = rot_dim // 2
        u = lane % period
        rot = u < rotary_lanes
        inv_freq = ROPE_THETA ** (-jnp.arange(half, dtype=F32) * (2.0 / rot_dim))
        ang = jnp.arange(seq, dtype=F32)[:, None] * inv_freq[u % half][None, :]
        cos, sin = jnp.cos(ang), jnp.sin(ang)
        return [jnp.where(rot, cos, 1.0),
                jnp.where(rot & first, -sin, jnp.where(rot & ~first, sin, 0.0))]

    return jnp.concatenate(tables(MLA_ROPE_DIM, 2 * MLA_ROPE_HALF, HALF_LANES)
                           + tables(PARTIAL_ROT_DIM, DIL_ROT_HALF, DIL_HEAD_HALF), axis=-1)


def _reorder_w_in(w_in):
    c_q = w_in[:, :MLA_Q_LORA]
    c_kv = w_in[:, MLA_Q_LORA:MLA_Q_LORA + MLA_KV_LORA]
    k_r = w_in[:, MLA_Q_LORA + MLA_KV_LORA:MLA_Q_LORA + MLA_KV_LORA + MLA_ROPE_DIM]
    rest = w_in[:, MLA_Q_LORA + MLA_KV_LORA + MLA_ROPE_DIM:]
    d_in = w_in.shape[0]
    pad = np.zeros((d_in, HALF_LANES - 2 * MLA_ROPE_HALF), w_in.dtype)
    k_x1, k_x2 = k_r[:, :MLA_ROPE_HALF], k_r[:, MLA_ROPE_HALF:]
    k_block = np.concatenate([k_x1, k_x1, pad, k_x2, k_x2, pad], axis=-1)
    q_d, k_d, v_d = (rest[:, i * D_DIL_OUT:(i + 1) * D_DIL_OUT] for i in range(3))
    return np.concatenate([c_q, c_kv, k_block, _dil_rotary_layout(q_d),
                           _dil_rotary_layout(k_d), v_d], axis=-1)


def _dil_rotary_layout(w):
    w = w.reshape(w.shape[0], N_DIL_HEADS // 2, 2, HEAD_DIM)
    n_pass = DIL_HEAD_HALF - DIL_ROT_HALF
    x1 = w[..., :DIL_ROT_HALF]
    x2 = w[..., DIL_ROT_HALF:PARTIAL_ROT_DIM]
    p1 = w[..., PARTIAL_ROT_DIM:PARTIAL_ROT_DIM + n_pass]
    p2 = w[..., PARTIAL_ROT_DIM + n_pass:]
    halves = [np.concatenate([x[:, :, 0], p[:, :, 0], x[:, :, 1], p[:, :, 1]], axis=-1)
              for x, p in ((x1, p1), (x2, p2))]
    return np.concatenate(halves, axis=-1).reshape(w.shape[0], D_DIL_OUT)


def _reorder_w_uq(w_uq):
    rows = w_uq.shape[0]
    w = w_uq.reshape(rows, N_PAIRS, 2, MLA_NOPE_DIM + MLA_ROPE_DIM)
    nope = w[..., :MLA_NOPE_DIM].reshape(rows, N_PAIRS, 2 * MLA_NOPE_DIM)
    rope = w[..., MLA_NOPE_DIM:]
    pad = np.zeros((rows, N_PAIRS, HALF_LANES - 2 * MLA_ROPE_HALF), w_uq.dtype)
    x1, x2 = rope[..., :MLA_ROPE_HALF], rope[..., MLA_ROPE_HALF:]
    block = [x1[:, :, 0], x1[:, :, 1], pad, x2[:, :, 0], x2[:, :, 1], pad]
    return np.concatenate([nope] + block, axis=-1).reshape(rows, N_PAIRS * MLA_PAIR_W)


def _reorder_w_ukv(w_ukv):
    rows = w_ukv.shape[0]
    w = w_ukv.reshape(rows, N_MLA_HEADS, MLA_NOPE_DIM + MLA_V_DIM)
    k_nope = w[..., :MLA_NOPE_DIM].reshape(rows, D_MLA_OUT)
    v = w[..., MLA_NOPE_DIM:].reshape(rows, D_MLA_OUT)
    return np.concatenate([k_nope, v], axis=-1)


def _column_source(layout, n_cols):
    idx = layout(np.arange(1, n_cols + 1)[None, :])[0]
    return np.where(idx == 0, n_cols, idx - 1).astype(np.int32)


def _permute_columns(w, src):
    w = w.astype(BF16)
    onehot = (lax.broadcasted_iota(jnp.int32, (w.shape[1], src.shape[0]), 0)
              == jnp.asarray(src)[None, :])
    return jnp.dot(w, onehot.astype(BF16), preferred_element_type=F32).astype(BF16)


def kernel(x, ffn1_norm, ffn1_w_gate, ffn1_w_up, ffn1_w_down, mix_norm, w_in, mla_q_norm, mla_w_uq, mla_kv_norm, mla_w_ukv, mla_out_norm, dil_out_norm, w_out, ffn2_norm, ffn2_w_gate, ffn2_w_up, ffn2_w_down, final_norm):
    b, s, d = x.shape
    assert ffn1_norm.shape[0] == 1, "single-layer block: the final norm is fused into its last FFN"
    assert d == D_MODEL and s % MLA_Q_ROWS == 0 and s % (Q_BLOCK * DIL_PAIRS[-1][1]) == 0
    tab = _rope_lane_tables(s)
    x1 = _ffn1(x.reshape(b * s, d), ffn1_norm[0][None], ffn1_w_gate[0].astype(BF16),
               ffn1_w_up[0].astype(BF16), ffn1_w_down[0].astype(BF16))
    q, kc, v, qd, kd, vd = _proj(
        x1.reshape(b, s, d), mix_norm[0][None],
        _permute_columns(w_in[0], _column_source(_reorder_w_in, w_in.shape[2])),
        mla_q_norm[0][None],
        _permute_columns(mla_w_uq[0], _column_source(_reorder_w_uq, mla_w_uq.shape[2])),
        mla_kv_norm[0][None],
        _permute_columns(mla_w_ukv[0], _column_source(_reorder_w_ukv, mla_w_ukv.shape[2])), tab)
    o_mla = _mla(q, kc, v).reshape(b * s, D_MLA_OUT)
    o_dil = _dilated(qd, kd, vd).reshape(b * s, D_DIL_OUT)
    y = _ffn2(x1, o_mla, o_dil, mla_out_norm[0][None], dil_out_norm[0][None],
              w_out[0].astype(BF16), ffn2_norm[0][None],
              ffn2_w_gate[0].astype(BF16), ffn2_w_up[0].astype(BF16),
              ffn2_w_down[0].astype(BF16), final_norm[None])
    return y.reshape(b, s, d)
```

```python
import functools

import numpy as np
import jax
import jax.numpy as jnp
from jax import lax
from jax.experimental import pallas as pl
from jax.experimental.pallas import tpu as pltpu

F32 = jnp.float32
BF16 = jnp.bfloat16

D_MODEL = 1024
HEAD_DIM = 64
N_MLA_HEADS = 8
N_DIL_HEADS = 8
MLA_NOPE_DIM = 64
MLA_ROPE_DIM = 32
MLA_V_DIM = 64
MLA_Q_LORA = 384
MLA_KV_LORA = 256
D_MLA_OUT = N_MLA_HEADS * MLA_V_DIM
D_DIL_OUT = N_DIL_HEADS * HEAD_DIM
DIL_PAIRS = ((128, 1), (512, 4), (2048, 16))
PARTIAL_ROT_DIM = 16
ROPE_THETA = 500000.0
D_FF = 2816
Q_BLOCK = 128
NORM_EPS = 1e-6
MASK_VALUE = -1e30
HALF_WINDOW = 64
assert all(w // (2 * d) == HALF_WINDOW for w, d in DIL_PAIRS)
LOG2_E = 1.4426950408889634

LANES = 128
V7X_VMEM_LIMIT_BYTES = 56 * 1024 * 1024

FFN_ROWS = 1024
FFN_HIDDEN_CHUNK = 1024
PROJ_ROWS = 1024
MLA_Q_ROWS = 512
MLA_KEY_TILE = 256
MLA_PAIRS_PER_STEP = 2
PAIR = 2 * HEAD_DIM
N_PAIRS = N_MLA_HEADS // 2
MLA_PAIR_W = 2 * LANES
HALF_LANES = LANES // 2
MLA_ROPE_HALF = MLA_ROPE_DIM // 2
DIL_ROT_HALF = PARTIAL_ROT_DIM // 2
DIL_HEAD_HALF = HEAD_DIM // 2

_OFF_CQ = 0
_OFF_CKV = _OFF_CQ + MLA_Q_LORA
_OFF_KR = _OFF_CKV + MLA_KV_LORA
_OFF_QD = _OFF_KR + LANES
_OFF_KD = _OFF_QD + D_DIL_OUT
_OFF_VD = _OFF_KD + D_DIL_OUT
_PROJ_W = _OFF_VD + D_DIL_OUT


def _rms(x, g):
    ms = jnp.mean(x * x, axis=-1, keepdims=True)
    return x * lax.rsqrt(ms + NORM_EPS) * g


def _rope(x, c, s):
    return x * c + pltpu.roll(x, LANES // 2, 1) * s


def _swiglu_residual(x, g_ref, wg_ref, wu_ref, wd_ref):
    h = _rms(x, g_ref[...]).astype(BF16)
    out = None
    for lo in range(0, D_FF, FFN_HIDDEN_CHUNK):
        hi = min(lo + FFN_HIDDEN_CHUNK, D_FF)
        gate = jnp.dot(h, wg_ref[:, lo:hi], preferred_element_type=F32)
        up = jnp.dot(h, wu_ref[:, lo:hi], preferred_element_type=F32)
        act = (gate * jax.nn.sigmoid(gate) * up).astype(BF16)
        part = jnp.dot(act, wd_ref[lo:hi, :], preferred_element_type=F32)
        out = part if out is None else out + part
    return x + 0.5 * out


def _ffn1_kernel(x_ref, g_ref, wg_ref, wu_ref, wd_ref, o_ref):
    o_ref[...] = _swiglu_residual(x_ref[...], g_ref, wg_ref, wu_ref, wd_ref)


def _ffn2_kernel(x_ref, om_ref, od_ref, gm_ref, gd_ref, wo_ref,
                 g_ref, wg_ref, wu_ref, wd_ref, gf_ref, o_ref):
    nm = _rms(om_ref[...].astype(F32), gm_ref[...]).astype(BF16)
    nd = _rms(od_ref[...].astype(F32), gd_ref[...]).astype(BF16)
    x = (x_ref[...]
         + jnp.dot(nm, wo_ref[:D_MLA_OUT, :], preferred_element_type=F32)
         + jnp.dot(nd, wo_ref[D_MLA_OUT:, :], preferred_element_type=F32))
    o_ref[...] = _rms(_swiglu_residual(x, g_ref, wg_ref, wu_ref, wd_ref), gf_ref[...])


def _resident(shape):
    return pl.BlockSpec(shape, lambda *_: (0,) * len(shape), pipeline_mode=pl.Buffered(1))


def _ffn1(x2d, g, wg, wu, wd):
    n = x2d.shape[0]
    row = pl.BlockSpec((FFN_ROWS, D_MODEL), lambda i: (i, 0))
    return pl.pallas_call(
        _ffn1_kernel,
        grid=(n // FFN_ROWS,),
        in_specs=[row, _resident((1, D_MODEL)), _resident((D_MODEL, D_FF)),
                  _resident((D_MODEL, D_FF)), _resident((D_FF, D_MODEL))],
        out_specs=row,
        out_shape=jax.ShapeDtypeStruct((n, D_MODEL), F32),
        compiler_params=pltpu.CompilerParams(
            dimension_semantics=("parallel",), vmem_limit_bytes=V7X_VMEM_LIMIT_BYTES),
        name="ffn1",
    )(x2d, g, wg, wu, wd)


def _ffn2(x2d, om, od, gm, gd, wo, g, wg, wu, wd, gf):
    n = x2d.shape[0]
    row = pl.BlockSpec((FFN_ROWS, D_MODEL), lambda i: (i, 0))
    half_row = pl.BlockSpec((FFN_ROWS, D_MLA_OUT), lambda i: (i, 0))
    return pl.pallas_call(
        _ffn2_kernel,
        grid=(n // FFN_ROWS,),
        in_specs=[row, half_row, half_row,
                  _resident((1, D_MLA_OUT)), _resident((1, D_DIL_OUT)),
                  _resident((D_MLA_OUT + D_DIL_OUT, D_MODEL)),
                  _resident((1, D_MODEL)), _resident((D_MODEL, D_FF)),
                  _resident((D_MODEL, D_FF)), _resident((D_FF, D_MODEL)),
                  _resident((1, D_MODEL))],
        out_specs=row,
        out_shape=jax.ShapeDtypeStruct((n, D_MODEL), F32),
        compiler_params=pltpu.CompilerParams(
            dimension_semantics=("parallel",), vmem_limit_bytes=V7X_VMEM_LIMIT_BYTES),
        name="ffn2",
    )(x2d, om, od, gm, gd, wo, g, wg, wu, wd, gf)


def _proj_kernel(x_ref, g_ref, win_ref, gq_ref, wuq_ref, gkv_ref, wukv_ref, tab_ref,
                 q_ref, kc_ref, v_ref, qd_ref, kd_ref, vd_ref):
    h = _rms(x_ref[...], g_ref[...]).astype(BF16)
    lat = jnp.dot(h, win_ref[:, :_OFF_QD], preferred_element_type=F32)
    dil = jnp.dot(h, win_ref[:, _OFF_QD:], preferred_element_type=F32)

    cm, sm, cd, sd = (tab_ref[:, i * LANES:(i + 1) * LANES] for i in range(4))
    rope_m = functools.partial(_rope, c=cm, s=sm)
    rope_d = functools.partial(_rope, c=cd, s=sd)

    cq = _rms(lat[:, _OFF_CQ:_OFF_CQ + MLA_Q_LORA], gq_ref[...]).astype(BF16)
    ckv = _rms(lat[:, _OFF_CKV:_OFF_CKV + MLA_KV_LORA], gkv_ref[...]).astype(BF16)
    q = jnp.dot(cq, wuq_ref[...], preferred_element_type=F32)

    d_scale = LOG2_E * HEAD_DIM ** -0.5
    for p in range(N_DIL_HEADS // 2):
        lo = p * PAIR
        qd_ref[:, lo:lo + PAIR] = rope_d(dil[:, lo:lo + PAIR]) * d_scale

    kv = jnp.dot(ckv, wukv_ref[...], preferred_element_type=F32)

    for p in range(N_DIL_HEADS // 2):
        lo = D_DIL_OUT + p * PAIR
        kd_ref[:, p * PAIR:(p + 1) * PAIR] = rope_d(dil[:, lo:lo + PAIR])
    vd_ref[...] = dil[:, 2 * D_DIL_OUT:]

    q_scale = LOG2_E * (MLA_NOPE_DIM + MLA_ROPE_DIM) ** -0.5
    for p in range(N_PAIRS):
        lo = p * MLA_PAIR_W
        q_ref[:, lo:lo + LANES] = (q[:, lo:lo + LANES] * q_scale).astype(BF16)
        q_ref[:, lo + LANES:lo + 2 * LANES] = (
            rope_m(q[:, lo + LANES:lo + 2 * LANES]) * q_scale).astype(BF16)

    k_rope = rope_m(lat[:, _OFF_KR:_OFF_KR + LANES]).astype(BF16)
    for p in range(N_PAIRS):
        lo = p * MLA_PAIR_W
        kc_ref[:, lo:lo + LANES] = kv[:, p * PAIR:(p + 1) * PAIR].astype(BF16)
        kc_ref[:, lo + LANES:lo + 2 * LANES] = k_rope
    v_ref[...] = kv[:, D_MLA_OUT:].astype(BF16)


def _proj(x3d, g, win, gq, wuq, gkv, wukv, tab):
    b, s, _ = x3d.shape

    def tok(width):
        return pl.BlockSpec((None, PROJ_ROWS, width), lambda si, bi: (bi, si, 0))

    def out(width, dtype):
        return jax.ShapeDtypeStruct((b, s, width), dtype)

    return pl.pallas_call(
        _proj_kernel,
        grid=(s // PROJ_ROWS, b),
        in_specs=[tok(D_MODEL), _resident((1, D_MODEL)), _resident((D_MODEL, _PROJ_W)),
                  _resident((1, MLA_Q_LORA)), _resident((MLA_Q_LORA, N_PAIRS * MLA_PAIR_W)),
                  _resident((1, MLA_KV_LORA)), _resident((MLA_KV_LORA, 2 * D_MLA_OUT)),
                  pl.BlockSpec((PROJ_ROWS, 4 * LANES), lambda si, bi: (si, 0))],
        out_specs=[tok(N_PAIRS * MLA_PAIR_W), tok(N_PAIRS * MLA_PAIR_W), tok(D_MLA_OUT),
                   tok(D_DIL_OUT), tok(D_DIL_OUT), tok(D_DIL_OUT)],
        out_shape=[out(N_PAIRS * MLA_PAIR_W, BF16), out(N_PAIRS * MLA_PAIR_W, BF16),
                   out(D_MLA_OUT, BF16), out(D_DIL_OUT, F32), out(D_DIL_OUT, F32),
                   out(D_DIL_OUT, F32)],
        compiler_params=pltpu.CompilerParams(
            dimension_semantics=("parallel", "parallel"),
            vmem_limit_bytes=V7X_VMEM_LIMIT_BYTES),
        name="proj",
    )(x3d, g, win, gq, wuq, gkv, wukv, tab)


def _mla_kernel(q_ref, kc_ref, v_ref, o_ref, s_ref):
    s_len = kc_ref.shape[0]
    n_tiles = s_len // MLA_KEY_TILE
    lane_q = lax.broadcasted_iota(jnp.int32, (MLA_Q_ROWS, MLA_PAIR_W), 1)
    first_o = lax.broadcasted_iota(jnp.int32, (MLA_Q_ROWS, PAIR), 1) < MLA_V_DIM
    chains = [(p, c, h) for p in range(MLA_PAIRS_PER_STEP)
              for c in range(s_len // MLA_Q_ROWS) for h in range(2)]
    pair_values = {}

    def v_heads(p):
        if p not in pair_values:
            v = v_ref[:, p * PAIR:(p + 1) * PAIR]
            first_v = lax.broadcasted_iota(jnp.int32, v.shape, 1) < MLA_V_DIM
            one = jnp.ones_like(v)
            pair_values.clear()
            pair_values[p] = (jnp.where(first_v, v, one), jnp.where(first_v, one, v))
        return pair_values[p]

    def masked_q(p, c, h):
        q = q_ref[c * MLA_Q_ROWS:(c + 1) * MLA_Q_ROWS, p * MLA_PAIR_W:(p + 1) * MLA_PAIR_W]
        nope = (lane_q >= h * MLA_NOPE_DIM) & (lane_q < (h + 1) * MLA_NOPE_DIM)
        in_half = lane_q & (HALF_LANES - 1)
        rope = ((lane_q >= LANES) & (in_half >= h * MLA_ROPE_HALF)
                & (in_half < (h + 1) * MLA_ROPE_HALF))
        return jnp.where(nope | rope, q, jnp.zeros_like(q))

    def score_tile(slot, p, qh, t, run_max):
        keys = slice(t * MLA_KEY_TILE, (t + 1) * MLA_KEY_TILE)
        s_t = lax.dot_general(qh, kc_ref[keys, p * MLA_PAIR_W:(p + 1) * MLA_PAIR_W],
                              (((1,), (1,)), ((), ())), preferred_element_type=F32)
        s_ref[slot, :, keys] = s_t
        tile_max = functools.reduce(
            jnp.maximum, [s_t[:, i:i + LANES] for i in range(0, MLA_KEY_TILE, LANES)])
        return tile_max if run_max is None else jnp.maximum(run_max, tile_max)

    def pv_tile(slot, p, h, t, m, acc):
        keys = slice(t * MLA_KEY_TILE, (t + 1) * MLA_KEY_TILE)
        p_t = jnp.exp2((s_ref[slot, :, keys] - m).astype(BF16))
        d = jnp.dot(p_t, v_heads(p)[h][keys, :], preferred_element_type=F32)
        return d if acc is None else acc + d

    qh = masked_q(*chains[0])
    run_max = None
    for t in range(n_tiles):
        run_max = score_tile(0, chains[0][0], qh, t, run_max)
    m = jnp.max(run_max, axis=-1, keepdims=True)
    res = []
    for i, (p, c, h) in enumerate(chains):
        nxt = chains[i + 1] if i + 1 < len(chains) else None
        if nxt is not None:
            qh = masked_q(*nxt)
        run_max, acc = None, None
        for t in range(n_tiles):
            if nxt is not None:
                run_max = score_tile((i + 1) % 2, nxt[0], qh, t, run_max)
            acc = pv_tile(i % 2, p, h, t, m, acc)
        if nxt is not None:
            m = jnp.max(run_max, axis=-1, keepdims=True)
        res.append(acc)
        if h == 1:
            num = jnp.where(first_o, res[0], res[1])
            den = pltpu.roll(jnp.where(first_o, res[1], res[0]), MLA_V_DIM, 1)
            o_ref[c * MLA_Q_ROWS:(c + 1) * MLA_Q_ROWS, p * PAIR:(p + 1) * PAIR] = (
                num / den).astype(o_ref.dtype)
            res = []


def _mla(q, kc, v):
    b, s, _ = q.shape
    n = MLA_PAIRS_PER_STEP
    return pl.pallas_call(
        _mla_kernel,
        grid=(b, N_PAIRS // MLA_PAIRS_PER_STEP),
        in_specs=[pl.BlockSpec((None, s, n * MLA_PAIR_W), lambda bi, p: (bi, 0, p)),
                  pl.BlockSpec((None, s, n * MLA_PAIR_W), lambda bi, p: (bi, 0, p)),
                  pl.BlockSpec((None, s, n * PAIR), lambda bi, p: (bi, 0, p))],
        out_specs=pl.BlockSpec((None, s, n * PAIR), lambda bi, p: (bi, 0, p)),
        out_shape=jax.ShapeDtypeStruct((b, s, D_MLA_OUT), BF16),
        scratch_shapes=[pltpu.VMEM((2, MLA_Q_ROWS, s), F32)],
        compiler_params=pltpu.CompilerParams(
            dimension_semantics=("parallel", "parallel"),
            vmem_limit_bytes=V7X_VMEM_LIMIT_BYTES),
        name="mla",
    )(q, kc, v)


DIL_SPAN = Q_BLOCK + 2 * HALF_WINDOW
DIL_MID = DIL_PAIRS[1][1]
assert DIL_PAIRS[0][1] == 1 and DIL_PAIRS[2][1] == DIL_MID * DIL_MID


def _band_bias():
    row = np.arange(Q_BLOCK)[:, None]
    col = np.arange(DIL_SPAN)[None, :]
    kinds = [np.abs(col - row - off) <= HALF_WINDOW
             for off in (0, HALF_WINDOW, 2 * HALF_WINDOW)]
    bias = [np.where(k, 0.0, MASK_VALUE) for k in kinds]
    short = bias[0][:, :Q_BLOCK]
    rows = [np.concatenate([b, b], axis=1) for b in bias]
    rows.append(np.concatenate([short, short, np.zeros_like(bias[0])], axis=1))
    return np.stack(rows).astype(np.float32)


def _band_scores(q, k_heads, bias):
    return lax.dot_general(q, k_heads, (((1,), (1,)), ((), ())),
                           preferred_element_type=F32) + bias


def _band_max(s):
    span = s.shape[1] // 2
    return (jnp.max(s[:, :span], axis=-1, keepdims=True),
            jnp.max(s[:, span:], axis=-1, keepdims=True))


def _band_probs(s, m):
    span = s.shape[1] // 2
    return jnp.concatenate([jnp.exp2((s[:, :span] - m[0]).astype(BF16)),
                            jnp.exp2((s[:, span:] - m[1]).astype(BF16))], axis=1)


def _band_finish(p, m, v_heads):
    res = jnp.dot(p, v_heads, preferred_element_type=F32)
    first = lax.broadcasted_iota(jnp.int32, (Q_BLOCK, PAIR), 1) < HEAD_DIM
    m_lanes = jnp.where(first, jnp.broadcast_to(m[0], (Q_BLOCK, PAIR)),
                        jnp.broadcast_to(m[1], (Q_BLOCK, PAIR)))
    return res[:, :PAIR], res[:, PAIR:], m_lanes


def _dil_kernel(q_ref, k_ref, v_ref, bias_ref, o_ref, mid_in_ref, nat_ref, mid_ref):
    s_len = q_ref.shape[0]
    mid_len = s_len // DIL_MID
    far_len = mid_len // DIL_MID
    lane = lax.broadcasted_iota(jnp.int32, (1, PAIR), 1)
    first = (lane & (HALF_LANES - 1)) < DIL_HEAD_HALF
    first_v = lane < HEAD_DIM

    for t, ref in enumerate((q_ref, k_ref, v_ref)):
        for r in range(DIL_MID):
            mid_in_ref[t, r] = ref[pl.ds(r, mid_len, stride=DIL_MID), :]

    groups = []

    def write_near(j, vals):
        for t, val in enumerate(vals):
            nat_ref[t, j * Q_BLOCK:(j + 1) * Q_BLOCK, :] = val
    groups.append((lambda: (q_ref[...], k_ref[...], v_ref[...]), s_len, write_near))

    for r in range(DIL_MID):
        def write_mid(j, vals, r=r):
            for t, val in enumerate(vals):
                mid_ref[0, t, r, j * Q_BLOCK:(j + 1) * Q_BLOCK, :] = val
        groups.append((lambda r=r: tuple(mid_in_ref[t, r] for t in range(3)), mid_len, write_mid))

    for r in range(DIL_MID):
        for rr in range(DIL_MID):
            sl = pl.ds(rr, far_len, stride=DIL_MID)
            def write_far(j, vals, r=r, sl=sl):
                for t, val in enumerate(vals):
                    mid_ref[1, t, r, sl, :] = val
            groups.append((lambda r=r, sl=sl: tuple(mid_in_ref[t, r, sl, :] for t in range(3)),
                           far_len, write_far))

    blocks = [(g, j) for g, (_, sub_len, _) in enumerate(groups)
              for j in range(sub_len // Q_BLOCK)]
    heavy = [blk for blk in blocks if groups[blk[0]][1] > Q_BLOCK]
    light = [blk for blk in blocks if groups[blk[0]][1] <= Q_BLOCK]
    per = len(heavy) // len(light)
    blocks = []
    for i, blk in enumerate(light):
        blocks += heavy[i * per:(i + 1) * per] + [blk]
    blocks += heavy[len(light) * per:]
    operands = {}

    def scores(g, j):
        load, sub_len, _ = groups[g]
        if g not in operands:
            q, k, v = load()
            one_a = jnp.broadcast_to(first_v.astype(F32), v.shape)
            operands[g] = (
                q.astype(BF16),
                jnp.where(first, k, 0.0).astype(BF16), jnp.where(first, 0.0, k).astype(BF16),
                jnp.concatenate([jnp.where(first_v, v, 0.0), one_a], axis=1).astype(BF16),
                jnp.concatenate([jnp.where(first_v, 0.0, v), 1.0 - one_a], axis=1).astype(BF16))
        qb, ka, kb, va, vb = operands[g]
        span = min(DIL_SPAN, sub_len)
        q0 = j * Q_BLOCK
        k0 = min(max(q0 - HALF_WINDOW, 0), sub_len - span)
        kind = 3 if span < DIL_SPAN else (q0 - k0) // HALF_WINDOW
        keys = slice(k0, k0 + span)
        k_heads = jnp.concatenate([ka[keys], kb[keys]], axis=0)
        v_heads = jnp.concatenate([va[keys], vb[keys]], axis=0)
        return (_band_scores(qb[q0:q0 + Q_BLOCK], k_heads, bias_ref[kind][:, :2 * span]),
                v_heads)

    n = len(blocks)
    sc = {i: scores(*blocks[i]) for i in range(min(3, n))}
    mx = {i: _band_max(sc[i][0]) for i in range(min(2, n))}
    pr = {0: _band_probs(sc[0][0], mx[0])}
    for i, (g, j) in enumerate(blocks):
        if i + 3 < n:
            sc[i + 3] = scores(*blocks[i + 3])
        if i + 2 < n:
            mx[i + 2] = _band_max(sc[i + 2][0])
        if i + 1 < n:
            pr[i + 1] = _band_probs(sc[i + 1][0], mx[i + 1])
        groups[g][2](j, _band_finish(pr.pop(i), mx.pop(i), sc.pop(i)[1]))

    for r in range(DIL_MID):
        nat = pl.ds(r, mid_len, stride=DIL_MID)
        accs = (nat_ref[0, nat, :], mid_ref[0, 0, r], mid_ref[1, 0, r])
        ls = (nat_ref[1, nat, :], mid_ref[0, 1, r], mid_ref[1, 1, r])
        ms = (nat_ref[2, nat, :], mid_ref[0, 2, r], mid_ref[1, 2, r])
        top = jnp.maximum(jnp.maximum(ms[0], ms[1]), ms[2])
        es = [jnp.exp2(m - top) for m in ms]
        num = es[0] * accs[0] + es[1] * accs[1] + es[2] * accs[2]
        den = es[0] * ls[0] + es[1] * ls[1] + es[2] * ls[2]
        o_ref[nat, :] = num / den


def _dilated(qd, kd, vd):
    b, s, _ = qd.shape
    blk = pl.BlockSpec((None, s, PAIR), lambda bi, p: (bi, 0, p))
    bias = jnp.asarray(_band_bias())
    return pl.pallas_call(
        _dil_kernel,
        grid=(b, N_DIL_HEADS // 2),
        in_specs=[blk, blk, blk, _resident(bias.shape)],
        out_specs=blk,
        out_shape=jax.ShapeDtypeStruct((b, s, D_DIL_OUT), F32),
        scratch_shapes=[pltpu.VMEM((3, DIL_MID, s // DIL_MID, PAIR), F32),
                        pltpu.VMEM((3, s, PAIR), F32),
                        pltpu.VMEM((2, 3, DIL_MID, s // DIL_MID, PAIR), F32)],
        compiler_params=pltpu.CompilerParams(
            dimension_semantics=("parallel", "parallel"),
            vmem_limit_bytes=V7X_VMEM_LIMIT_BYTES),
        name="dilated",
    )(qd, kd, vd, bias)


def _rope_lane_tables(seq):
    lane = np.arange(LANES)
    first = lane < HALF_LANES

    def tables(rot_dim, rotary_lanes, period):
        half = rot_dim // 2
        u = lane % period
        rot = u < rotary_lanes
        inv_freq = ROPE_THETA ** (-jnp.arange(half, dtype=F32) * (2.0 / rot_dim))
        ang = jnp.arange(seq, dtype=F32)[:, None] * inv_freq[u % half][None, :]
        cos, sin = jnp.cos(ang), jnp.sin(ang)
        return [jnp.where(rot, cos, 1.0),
                jnp.where(rot & first, -sin, jnp.where(rot & ~first, sin, 0.0))]

    return jnp.concatenate(tables(MLA_ROPE_DIM, 2 * MLA_ROPE_HALF, HALF_LANES)
                           + tables(PARTIAL_ROT_DIM, DIL_ROT_HALF, DIL_HEAD_HALF), axis=-1)


def _reorder_w_in(w_in):
    c_q = w_in[:, :MLA_Q_LORA]
    c_kv = w_in[:, MLA_Q_LORA:MLA_Q_LORA + MLA_KV_LORA]
    k_r = w_in[:, MLA_Q_LORA + MLA_KV_LORA:MLA_Q_LORA + MLA_KV_LORA + MLA_ROPE_DIM]
    rest = w_in[:, MLA_Q_LORA + MLA_KV_LORA + MLA_ROPE_DIM:]
    d_in = w_in.shape[0]
    pad = np.zeros((d_in, HALF_LANES - 2 * MLA_ROPE_HALF), w_in.dtype)
    k_x1, k_x2 = k_r[:, :MLA_ROPE_HALF], k_r[:, MLA_ROPE_HALF:]
    k_block = np.concatenate([k_x1, k_x1, pad, k_x2, k_x2, pad], axis=-1)
    q_d, k_d, v_d = (rest[:, i * D_DIL_OUT:(i + 1) * D_DIL_OUT] for i in range(3))
    return np.concatenate([c_q, c_kv, k_block, _dil_rotary_layout(q_d),
                           _dil_rotary_layout(k_d), v_d], axis=-1)


def _dil_rotary_layout(w):
    w = w.reshape(w.shape[0], N_DIL_HEADS // 2, 2, HEAD_DIM)
    n_pass = DIL_HEAD_HALF - DIL_ROT_HALF
    x1 = w[..., :DIL_ROT_HALF]
    x2 = w[..., DIL_ROT_HALF:PARTIAL_ROT_DIM]
    p1 = w[..., PARTIAL_ROT_DIM:PARTIAL_ROT_DIM + n_pass]
    p2 = w[..., PARTIAL_ROT_DIM + n_pass:]
    halves = [np.concatenate([x[:, :, 0], p[:, :, 0], x[:, :, 1], p[:, :, 1]], axis=-1)
              for x, p in ((x1, p1), (x2, p2))]
    return np.concatenate(halves, axis=-1).reshape(w.shape[0], D_DIL_OUT)


def _reorder_w_uq(w_uq):
    rows = w_uq.shape[0]
    w = w_uq.reshape(rows, N_PAIRS, 2, MLA_NOPE_DIM + MLA_ROPE_DIM)
    nope = w[..., :MLA_NOPE_DIM].reshape(rows, N_PAIRS, 2 * MLA_NOPE_DIM)
    rope = w[..., MLA_NOPE_DIM:]
    pad = np.zeros((rows, N_PAIRS, HALF_LANES - 2 * MLA_ROPE_HALF), w_uq.dtype)
    x1, x2 = rope[..., :MLA_ROPE_HALF], rope[..., MLA_ROPE_HALF:]
    block = [x1[:, :, 0], x1[:, :, 1], pad, x2[:, :, 0], x2[:, :, 1], pad]
    return np.concatenate([nope] + block, axis=-1).reshape(rows, N_PAIRS * MLA_PAIR_W)


def _reorder_w_ukv(w_ukv):
    rows = w_ukv.shape[0]
    w = w_ukv.reshape(rows, N_MLA_HEADS, MLA_NOPE_DIM + MLA_V_DIM)
    k_nope = w[..., :MLA_NOPE_DIM].reshape(rows, D_MLA_OUT)
    v = w[..., MLA_NOPE_DIM:].reshape(rows, D_MLA_OUT)
    return np.concatenate([k_nope, v], axis=-1)


def _column_source(layout, n_cols):
    idx = layout(np.arange(1, n_cols + 1)[None, :])[0]
    return np.where(idx == 0, n_cols, idx - 1).astype(np.int32)


def _permute_columns(w, src):
    w = w.astype(BF16)
    onehot = (lax.broadcasted_iota(jnp.int32, (w.shape[1], src.shape[0]), 0)
              == jnp.asarray(src)[None, :])
    return jnp.dot(w, onehot.astype(BF16), preferred_element_type=F32).astype(BF16)


def kernel(x, ffn1_norm, ffn1_w_gate, ffn1_w_up, ffn1_w_down, mix_norm, w_in, mla_q_norm, mla_w_uq, mla_kv_norm, mla_w_ukv, mla_out_norm, dil_out_norm, w_out, ffn2_norm, ffn2_w_gate, ffn2_w_up, ffn2_w_down, final_norm):
    b, s, d = x.shape
    assert ffn1_norm.shape[0] == 1, "single-layer block: the final norm is fused into its last FFN"
    assert d == D_MODEL and s % MLA_Q_ROWS == 0 and s % (Q_BLOCK * DIL_PAIRS[-1][1]) == 0
    tab = _rope_lane_tables(s)
    x1 = _ffn1(x.reshape(b * s, d), ffn1_norm[0][None], ffn1_w_gate[0].astype(BF16),
               ffn1_w_up[0].astype(BF16), ffn1_w_down[0].astype(BF16))
    q, kc, v, qd, kd, vd = _proj(
        x1.reshape(b, s, d), mix_norm[0][None],
        _permute_columns(w_in[0], _column_source(_reorder_w_in, w_in.shape[2])),
        mla_q_norm[0][None],
        _permute_columns(mla_w_uq[0], _column_source(_reorder_w_uq, mla_w_uq.shape[2])),
        mla_kv_norm[0][None],
        _permute_columns(mla_w_ukv[0], _column_source(_reorder_w_ukv, mla_w_ukv.shape[2])), tab)
    o_mla = _mla(q, kc, v).reshape(b * s, D_MLA_OUT)
    o_dil = _dilated(qd, kd, vd).reshape(b * s, D_DIL_OUT)
    y = _ffn2(x1, o_mla, o_dil, mla_out_norm[0][None], dil_out_norm[0][None],
              w_out[0].astype(BF16), ffn2_norm[0][None],
              ffn2_w_gate[0].astype(BF16), ffn2_w_up[0].astype(BF16),
              ffn2_w_down[0].astype(BF16), final_norm[None])
    return y.reshape(b, s, d)
```
